```python
import jax, jax.numpy as jnp
from jax import lax
import numpy as np

D_MODEL = 1024
BATCH = 2
SEQ = 16384
DEPTH = 2

EPS = 1e-6
GRID_W = 64
D_MIX = D_MODEL
HEAD_DIM = 64
ATT_HEADS = (D_MIX // 2) // HEAD_DIM
ATT_KV_HEADS = 2
ATT_GROUP = ATT_HEADS // ATT_KV_HEADS
ATT_WIDTH = ATT_HEADS * HEAD_DIM
KV_WIDTH = ATT_KV_HEADS * HEAD_DIM
ATT_SCALE = HEAD_DIM ** -0.5
Q_BLOCK = 128
ROPE_THETA = 10000.0
ROPE_AXIS_DIM = HEAD_DIM // 2
HGRN_WIDTH = D_MIX // 4
HGRN_DK = 64
HGRN_DV = 64
HGRN_HEADS = HGRN_WIDTH // HGRN_DV
HGRN_CHUNK = 16
CONV_WIDTH = D_MIX - ATT_WIDTH - HGRN_WIDTH
CONV_KERNEL = 31
CONV_PAD = (CONV_KERNEL - 1) // 2
D_FF = 4 * D_MODEL
IN_COLS = ATT_WIDTH + 2 * KV_WIDTH + 5 * HGRN_WIDTH + 2 * CONV_WIDTH

kernel_name = "hybrid_parallel_heads_encoder"


def _in_split_points():
    sizes = [ATT_WIDTH, KV_WIDTH, KV_WIDTH, HGRN_WIDTH, HGRN_WIDTH, HGRN_WIDTH,
             HGRN_WIDTH, HGRN_WIDTH, CONV_WIDTH, CONV_WIDTH]
    points, acc = [], 0
    for s in sizes[:-1]:
        acc += s
        points.append(acc)
    return points


def rms_norm(x, g):
    xf = x.astype(jnp.float32)
    y = xf * lax.rsqrt(jnp.mean(xf * xf, axis=-1, keepdims=True) + EPS)
    return (y * g.astype(jnp.float32)).astype(x.dtype)


def layer_norm(x, g, b):
    xf = x.astype(jnp.float32)
    mu = jnp.mean(xf, axis=-1, keepdims=True)
    xc = xf - mu
    y = xc * lax.rsqrt(jnp.mean(xc * xc, axis=-1, keepdims=True) + EPS)
    return (y * g.astype(jnp.float32) + b.astype(jnp.float32)).astype(x.dtype)


def _rope_angles(pos):
    inv_freq = ROPE_THETA ** (-jnp.arange(0, ROPE_AXIS_DIM, 2, dtype=jnp.float32) / ROPE_AXIS_DIM)
    ang = pos.astype(jnp.float32)[:, None] * inv_freq[None, :]
    return jnp.cos(ang), jnp.sin(ang)


def _rotate(x, cos, sin):
    half = x.shape[-1] // 2
    x1, x2 = x[..., :half], x[..., half:]
    c, s = cos[None, :, None, :], sin[None, :, None, :]
    return jnp.concatenate([x1 * c - x2 * s, x1 * s + x2 * c], axis=-1)


def apply_axial_rope(x, row_cs, col_cs):
    xf = x.astype(jnp.float32)
    out = jnp.concatenate([_rotate(xf[..., :ROPE_AXIS_DIM], *row_cs),
                           _rotate(xf[..., ROPE_AXIS_DIM:], *col_cs)], axis=-1)
    return out.astype(x.dtype)


def block_attention(q, k, v):
    B, L, H, D = q.shape
    nb = L // Q_BLOCK
    qb = q.reshape(B, nb, Q_BLOCK, ATT_KV_HEADS, ATT_GROUP, D).transpose(1, 0, 2, 3, 4, 5)

    def one_block(q_blk):
        s = jnp.einsum("bqhgd,bkhd->bhgqk", q_blk, k).astype(jnp.float32) * ATT_SCALE
        p = jax.nn.softmax(s, axis=-1)
        return jnp.einsum("bhgqk,bkhd->bqhgd", p.astype(v.dtype), v)

    o = lax.map(one_block, qb)
    return o.transpose(1, 0, 2, 3, 4, 5).reshape(B, L, H * D)


def chunk_gated_recurrence(q, k, v, log_f):
    B, L, H, K = q.shape
    V = v.shape[-1]
    C = HGRN_CHUNK
    N = L // C

    def to_chunks(a):
        return a.reshape(B, N, C, H, a.shape[-1]).transpose(0, 3, 1, 2, 4)

    q, k, v, log_f = (to_chunks(a) for a in (q, k, v, log_f))
    b = jnp.cumsum(log_f, axis=3)
    causal = jnp.tril(jnp.ones((C, C), dtype=bool))[:, :, None]
    diff = b[:, :, :, :, None, :] - b[:, :, :, None, :, :]
    decay = jnp.where(causal, jnp.exp(jnp.where(causal, diff, 0.0)), 0.0)
    scores = jnp.einsum("bhnik,bhnjk,bhnijk->bhnij", q, k, decay)
    o_intra = jnp.einsum("bhnij,bhnjv->bhniv", scores, v)
    b_last = b[:, :, :, -1:, :]
    u = jnp.einsum("bhnjk,bhnjv->bhnkv", k * jnp.exp(b_last - b), v)
    chunk_decay = jnp.exp(b_last[:, :, :, 0, :])

    def step(S, inp):
        dec, uu = inp
        return dec[..., None] * S + uu, S

    S0 = jnp.zeros((B, H, K, V), q.dtype)
    _, S_prev = lax.scan(step, S0, (jnp.moveaxis(chunk_decay, 2, 0), jnp.moveaxis(u, 2, 0)))
    S_prev = jnp.moveaxis(S_prev, 0, 2)
    o_inter = jnp.einsum("bhnik,bhnkv->bhniv", q * jnp.exp(b), S_prev)
    return (o_intra + o_inter).transpose(0, 2, 3, 1, 4).reshape(B, L, H, V)


def _layer_lower_bounds(lb_param):
    p = jax.nn.softmax(lb_param.astype(jnp.float32), axis=0)
    return jnp.cumsum(p, axis=0) - p[0:1]


def _forget_gate(z, lb):
    zf = z.astype(jnp.float32)
    log_f = jax.nn.log_sigmoid(zf) + jnp.log1p(lb * jnp.exp(-zf))
    k = (1.0 - lb) * jax.nn.sigmoid(-zf)
    return log_f, k


def hgrn2_bidirectional(q_h, i_h, f_fw, f_bw, g_h, lb_fw, lb_bw, norm_g):
    B, L, _ = q_h.shape
    heads = lambda a: a.astype(jnp.float32).reshape(B, L, HGRN_HEADS, -1)
    q, v = heads(q_h), heads(i_h)
    lb_fw = lb_fw.reshape(HGRN_HEADS, HGRN_DK)
    lb_bw = lb_bw.reshape(HGRN_HEADS, HGRN_DK)
    logf_fw, k_fw = _forget_gate(heads(f_fw), lb_fw)
    logf_bw, k_bw = _forget_gate(heads(f_bw), lb_bw)
    flip = lambda a: jnp.flip(a, axis=1)
    o_fw = chunk_gated_recurrence(q, k_fw, v, logf_fw)
    o_bw = flip(chunk_gated_recurrence(flip(q), flip(k_bw), flip(v), flip(logf_bw)))
    o = rms_norm(o_fw + o_bw, norm_g.reshape(HGRN_HEADS, HGRN_DV))
    o = o.reshape(B, L, HGRN_WIDTH) * jax.nn.silu(g_h.astype(jnp.float32))
    return o.astype(q_h.dtype)


def conformer_conv(a, gate, w_dw, b_dw, ln_g, ln_b):
    u = a * jax.nn.sigmoid(gate)
    y = lax.conv_general_dilated(u, w_dw[:, None, :], window_strides=(1,),
                                 padding=[(CONV_PAD, CONV_PAD)],
                                 dimension_numbers=("NWC", "WIO", "NWC"),
                                 feature_group_count=CONV_WIDTH)
    y = layer_norm(y + b_dw, ln_g, ln_b)
    return jax.nn.silu(y)


def setup_inputs(seed: int = 0) -> dict:
    key = jax.random.key(seed)
    ks = jax.random.split(key, 16)
    n = lambda k, shape: jax.random.normal(k, shape, jnp.float32)
    return {
        "x": n(ks[0], (BATCH, SEQ, D_MODEL)),
        "w_in": n(ks[1], (DEPTH, D_MODEL, IN_COLS)) * D_MODEL ** -0.5,
        "w_out": n(ks[2], (DEPTH, D_MIX, D_MODEL)) * D_MIX ** -0.5,
        "norm_mix": 1.0 + 0.05 * n(ks[3], (DEPTH, D_MODEL)),
        "norm_mlp": 1.0 + 0.05 * n(ks[4], (DEPTH, D_MODEL)),
        "q_norm": 1.0 + 0.05 * n(ks[5], (DEPTH, HEAD_DIM)),
        "k_norm": 1.0 + 0.05 * n(ks[6], (DEPTH, HEAD_DIM)),
        "hgrn_lb_fwd": 0.5 * n(ks[7], (DEPTH, HGRN_WIDTH)),
        "hgrn_lb_bwd": 0.5 * n(ks[8], (DEPTH, HGRN_WIDTH)),
        "hgrn_norm": 1.0 + 0.05 * n(ks[9], (DEPTH, HGRN_WIDTH)),
        "conv_w": n(ks[10], (DEPTH, CONV_KERNEL, CONV_WIDTH)) * CONV_KERNEL ** -0.5,
        "conv_b": 0.02 * n(ks[11], (DEPTH, CONV_WIDTH)),
        "conv_ln_g": 1.0 + 0.05 * n(ks[12], (DEPTH, CONV_WIDTH)),
        "conv_ln_b": 0.02 * n(ks[13], (DEPTH, CONV_WIDTH)),
        "w_mlp_in": n(ks[14], (DEPTH, D_MODEL, D_FF)) * D_MODEL ** -0.5,
        "w_mlp_out": n(ks[15], (DEPTH, D_FF, D_MODEL)) * D_FF ** -0.5,
    }


def reference(x, w_in, w_out, norm_mix, norm_mlp, q_norm, k_norm, hgrn_lb_fwd, hgrn_lb_bwd,
              hgrn_norm, conv_w, conv_b, conv_ln_g, conv_ln_b, w_mlp_in, w_mlp_out):
    B, L, _ = x.shape
    rows = L // GRID_W
    row_idx = jnp.broadcast_to(jnp.arange(rows)[:, None], (rows, GRID_W)).reshape(L)
    col_idx = jnp.broadcast_to(jnp.arange(GRID_W)[None, :], (rows, GRID_W)).reshape(L)
    row_cs, col_cs = _rope_angles(row_idx), _rope_angles(col_idx)
    lb_fwd = _layer_lower_bounds(hgrn_lb_fwd)
    lb_bwd = _layer_lower_bounds(hgrn_lb_bwd)
    splits = _in_split_points()

    for l in range(DEPTH):
        h = rms_norm(x, norm_mix[l])
        z = h @ w_in[l]
        (q_a, k_a, v_a, q_h, i_h, f_fw, f_bw, g_h, a_c, g_c) = jnp.split(z, splits, axis=-1)

        q_a = rms_norm(q_a.reshape(B, L, ATT_HEADS, HEAD_DIM), q_norm[l])
        k_a = rms_norm(k_a.reshape(B, L, ATT_KV_HEADS, HEAD_DIM), k_norm[l])
        q_a = apply_axial_rope(q_a, row_cs, col_cs)
        k_a = apply_axial_rope(k_a, row_cs, col_cs)
        v_a = v_a.reshape(B, L, ATT_KV_HEADS, HEAD_DIM)
        o_att = block_attention(q_a, k_a, v_a)

        o_hg = hgrn2_bidirectional(q_h, i_h, f_fw, f_bw, g_h,
                                   lb_fwd[l], lb_bwd[l], hgrn_norm[l])

        o_cv = conformer_conv(a_c, g_c, conv_w[l], conv_b[l],
                              conv_ln_g[l], conv_ln_b[l])

        mix = jnp.concatenate([o_att, o_hg.astype(x.dtype), o_cv], axis=-1)
        x = x + mix @ w_out[l]

        h = rms_norm(x, norm_mlp[l])
        x = x + jnp.square(jax.nn.relu(h @ w_mlp_in[l])) @ w_mlp_out[l]
    return x
```

```python
import functools
import math

import jax
import jax.numpy as jnp
from jax import lax
from jax.experimental import pallas as pl
from jax.experimental.pallas import tpu as pltpu

F32 = jnp.float32
BF16 = jnp.bfloat16

EPS = 1e-6
GRID_W = 64
HEAD_DIM = 64
ATT_HEADS = 8
ATT_KV_HEADS = 2
ATT_GROUP = ATT_HEADS // ATT_KV_HEADS
ATT_WIDTH = ATT_HEADS * HEAD_DIM
KV_WIDTH = ATT_KV_HEADS * HEAD_DIM
ROPE_THETA = 10000.0
ROPE_AXIS_DIM = HEAD_DIM // 2
HGRN_WIDTH = 256
HGRN_HEADS = 4
HGRN_CHUNK = 16
CONV_WIDTH = 256
CONV_KERNEL = 31
CONV_PAD = (CONV_KERNEL - 1) // 2
CONV_HALO = 16

LANES = 128
LOG2E = 1.4426950408889634
VMEM_LIMIT = 56 * 1024 * 1024

_C_Q = 0
_C_K = _C_Q + ATT_WIDTH
_C_V = _C_K + KV_WIDTH
_C_HQ = _C_V + KV_WIDTH
_C_HI = _C_HQ + HGRN_WIDTH
_C_FF = _C_HI + HGRN_WIDTH
_C_FB = _C_FF + HGRN_WIDTH
_C_HG = _C_FB + HGRN_WIDTH
_C_CA = _C_HG + HGRN_WIDTH
_C_CG = _C_CA + CONV_WIDTH
IN_COLS = _C_CG + CONV_WIDTH


def _dot(a, b):
    return jnp.dot(a, b, preferred_element_type=F32)


def _split_dot(x, m, parts):
    acc = None
    r = x
    for _ in range(parts):
        t = r.astype(BF16)
        acc = _dot(t, m) if acc is None else acc + _dot(t, m)
        r = r - t.astype(F32)
    return acc


def _group_matrix(width, group, value):
    r = lax.broadcasted_iota(jnp.int32, (width, width), 0) // group
    c = lax.broadcasted_iota(jnp.int32, (width, width), 1) // group
    return jnp.where(r == c, value, 0.0).astype(BF16)


def _group_mean_matrix(width, group):
    return _group_matrix(width, group, 1.0 / group)


def _sigmoid(x):
    return 1.0 / (1.0 + jnp.exp(-x))


def _log_sigmoid(x):
    return jnp.minimum(x, 0.0) - jnp.log1p(jnp.exp(-jnp.abs(x)))


def _in_proj_kernel(x_ref, g_ref, w_ref, qn_ref, kn_ref, cos_ref, sin_ref, lbf_ref, lbb_ref,
                    q_out, kt_out, va_out, hq_out, hv_out, lff_out, kf_out, lfb_out, kb_out, sg_out, u_out,
                    *, layer):
    x = x_ref[...]
    ms = jnp.mean(x * x, axis=-1, keepdims=True)
    h = (x * lax.rsqrt(ms + EPS) * g_ref[...]).astype(BF16)

    def proj(lo, width):
        return _dot(h, w_ref[:, lo:lo + width])

    gmat = _group_mean_matrix(LANES, HEAD_DIM)
    cos = cos_ref[...]
    sin = sin_ref[...]
    lane = lax.broadcasted_iota(jnp.int32, (1, LANES), 1)
    first_half = (lane % ROPE_AXIS_DIM) < (ROPE_AXIS_DIM // 2)

    def norm_rope(z, gain, scale):
        zn = z * lax.rsqrt(_split_dot(z * z, gmat, 2) + EPS) * gain
        half = ROPE_AXIS_DIM // 2
        partner = jnp.where(first_half, pltpu.roll(zn, LANES - half, 1), pltpu.roll(zn, half, 1))
        return (zn * cos + partner * sin) * scale

    zq = proj(_C_Q, ATT_WIDTH)
    q_scale = HEAD_DIM ** -0.5 * LOG2E
    q_out[...] = jnp.concatenate(
        [norm_rope(zq[:, c * LANES:(c + 1) * LANES], qn_ref[...], q_scale) for c in range(ATT_WIDTH // LANES)],
        axis=1).astype(BF16)

    zk = norm_rope(proj(_C_K, KV_WIDTH), kn_ref[...], 1.0)
    zkt = jnp.transpose(zk).astype(BF16)
    for hh in range(ATT_KV_HEADS):
        kt_out[hh] = zkt[hh * HEAD_DIM:(hh + 1) * HEAD_DIM, :]

    zv = proj(_C_V, KV_WIDTH)
    low = lane < HEAD_DIM
    va_out[0] = jnp.where(low, zv, 1.0).astype(BF16)
    va_out[1] = jnp.where(low, pltpu.roll(zv, HEAD_DIM, 1), 1.0).astype(BF16)

    def lower_bound(lb_ref):
        p = lb_ref[...]
        p = jnp.exp(p - jnp.max(p, axis=0, keepdims=True))
        p = p / jnp.sum(p, axis=0, keepdims=True)
        lb = jnp.zeros((1, HGRN_WIDTH), F32)
        for i in range(1, layer + 1):
            lb = lb + p[i:i + 1, :]
        return lb

    def forget(z, lb):
        log_f = _log_sigmoid(z) + jnp.log1p(lb * jnp.exp(-z))
        return log_f, (1.0 - lb) * _sigmoid(-z)

    hq_out[...] = proj(_C_HQ, HGRN_WIDTH)
    hv_out[...] = proj(_C_HI, HGRN_WIDTH)
    lf, kk = forget(proj(_C_FF, HGRN_WIDTH), lower_bound(lbf_ref))
    lff_out[...] = lf
    kf_out[...] = kk
    lf, kk = forget(proj(_C_FB, HGRN_WIDTH), lower_bound(lbb_ref))
    lfb_out[...] = lf
    kb_out[...] = kk
    zg = proj(_C_HG, HGRN_WIDTH)
    sg_out[...] = zg * _sigmoid(zg)

    u_out[...] = proj(_C_CA, CONV_WIDTH) * _sigmoid(proj(_C_CG, CONV_WIDTH))


def _in_proj(x, g, w, qn, kn, cos, sin, lbf, lbb, *, layer, tm):
    B, L, D = x.shape
    nt = L // tm
    tok = lambda width: pl.BlockSpec((None, tm, width), lambda b, i: (b, i, 0))
    full = lambda a: pl.BlockSpec(a.shape, lambda b, i: (0,) * a.ndim)
    hg = jax.ShapeDtypeStruct((B, L, HGRN_WIDTH), F32)
    out_shape = (
        jax.ShapeDtypeStruct((B, L, ATT_WIDTH), BF16),
        jax.ShapeDtypeStruct((B, ATT_KV_HEADS, HEAD_DIM, L), BF16),
        jax.ShapeDtypeStruct((B, ATT_KV_HEADS, L, LANES), BF16),
        hg, hg, hg, hg, hg, hg, hg,
        jax.ShapeDtypeStruct((B, L, CONV_WIDTH), F32),
    )
    out_specs = (
        tok(ATT_WIDTH),
        pl.BlockSpec((None, ATT_KV_HEADS, HEAD_DIM, tm), lambda b, i: (b, 0, 0, i)),
        pl.BlockSpec((None, ATT_KV_HEADS, tm, LANES), lambda b, i: (b, 0, i, 0)),
        tok(HGRN_WIDTH), tok(HGRN_WIDTH), tok(HGRN_WIDTH), tok(HGRN_WIDTH), tok(HGRN_WIDTH), tok(HGRN_WIDTH),
        tok(HGRN_WIDTH), tok(CONV_WIDTH),
    )
    in_specs = [
        tok(D), full(g), full(w), full(qn), full(kn),
        pl.BlockSpec((tm, LANES), lambda b, i: (i, 0)),
        pl.BlockSpec((tm, LANES), lambda b, i: (i, 0)),
        full(lbf), full(lbb),
    ]
    return pl.pallas_call(
        functools.partial(_in_proj_kernel, layer=layer),
        grid=(B, nt),
        in_specs=in_specs,
        out_specs=out_specs,
        out_shape=out_shape,
        compiler_params=pltpu.CompilerParams(
            dimension_semantics=("parallel", "parallel"), vmem_limit_bytes=VMEM_LIMIT),
        name=f"in_proj_{layer}",
    )(x, g, w, qn, kn, cos, sin, lbf, lbb)


def _attn_kernel(q_ref, kt_ref, va_ref, o_ref, qs_ref, m_ref, acc_ref, *, tq, tk, nk):
    q = q_ref[...]
    qs_ref[...] = jnp.concatenate(
        [q[:, g * HEAD_DIM:(g + 1) * HEAD_DIM] for g in range(ATT_GROUP)], axis=0)
    m_ref[...] = jnp.full(m_ref.shape, -jnp.inf, F32)
    acc_ref[...] = jnp.zeros(acc_ref.shape, F32)

    def body(j, carry):
        k0 = pl.multiple_of(j * tk, tk)
        s = _dot(qs_ref[...], kt_ref[:, pl.ds(k0, tk)])
        m_prev = m_ref[...]
        m_new = jnp.maximum(m_prev, jnp.max(s, axis=1, keepdims=True))
        p = jnp.exp2(s - pltpu.repeat(m_new, tk // LANES, 1)).astype(BF16)
        acc_ref[...] = jnp.exp2(m_prev - m_new) * acc_ref[...] + _dot(p, va_ref[pl.ds(k0, tk), :])
        m_ref[...] = m_new
        return carry

    lax.fori_loop(0, nk, body, 0)
    acc = acc_ref[...]
    res = acc / pltpu.roll(acc, HEAD_DIM, 1)
    o_ref[...] = jnp.concatenate(
        [res[g * tq:(g + 1) * tq, :HEAD_DIM] for g in range(ATT_GROUP)], axis=1).astype(o_ref.dtype)


def _attention(q, kt, va, *, tq, tk):
    B, L, _ = q.shape
    gw = ATT_GROUP * HEAD_DIM
    return pl.pallas_call(
        functools.partial(_attn_kernel, tq=tq, tk=tk, nk=L // tk),
        grid=(B, ATT_KV_HEADS, L // tq),
        in_specs=[
            pl.BlockSpec((None, tq, gw), lambda b, h, i: (b, i, h)),
            pl.BlockSpec((None, None, HEAD_DIM, L), lambda b, h, i: (b, h, 0, 0)),
            pl.BlockSpec((None, None, L, LANES), lambda b, h, i: (b, h, 0, 0)),
        ],
        out_specs=pl.BlockSpec((None, tq, gw), lambda b, h, i: (b, i, h)),
        out_shape=jax.ShapeDtypeStruct((B, L, ATT_WIDTH), BF16),
        scratch_shapes=[
            pltpu.VMEM((ATT_GROUP * tq, HEAD_DIM), BF16),
            pltpu.VMEM((ATT_GROUP * tq, LANES), F32),
            pltpu.VMEM((ATT_GROUP * tq, LANES), F32),
        ],
        compiler_params=pltpu.CompilerParams(
            dimension_semantics=("parallel", "parallel", "parallel"), vmem_limit_bytes=VMEM_LIMIT),
        name="attention",
    )(q, kt, va)


def _hgrn_kernel(q_ref, v_ref, lf_ref, k_ref, o_ref, s_ref, *, T, reverse):
    C = HGRN_CHUNK
    W = HGRN_WIDTH

    @pl.when(pl.program_id(1) == 0)
    def _():
        s_ref[...] = jnp.zeros(s_ref.shape, F32)

    q = q_ref[...]
    v = v_ref[...]
    k = k_ref[...]
    lf = lf_ref[...]

    r = lax.broadcasted_iota(jnp.int32, (T, T), 0)
    c = lax.broadcasted_iota(jnp.int32, (T, T), 1)
    same = (r // C) == (c // C)
    tri = same & ((c >= r) if reverse else (c <= r))
    m_tri = jnp.where(tri, 1.0, 0.0).astype(BF16)
    m_all = jnp.where(same, 1.0, 0.0).astype(BF16)

    def left_split_dot(m, x):
        acc = None
        rem = x
        for _ in range(3):
            t = rem.astype(BF16)
            acc = _dot(m, t) if acc is None else acc + _dot(m, t)
            rem = rem - t.astype(F32)
        return acc

    b = left_split_dot(m_tri, lf)
    bl = left_split_dot(m_all, lf)

    pos = lax.broadcasted_iota(jnp.int32, (T, 1), 0) % C
    head_sum = _group_matrix(W, W // HGRN_HEADS, 1.0)
    acc = jnp.zeros((T, W), F32)
    for d in range(C):
        if d == 0:
            ks, bs, vs = k, b, v
        else:
            sh = (T - d) if reverse else d
            ks, bs, vs = (pltpu.roll(a, sh, 0) for a in (k, b, v))
        valid = (pos <= C - 1 - d) if reverse else (pos >= d)
        w = jnp.where(valid, q * ks * jnp.exp(b - bs), 0.0)
        acc = acc + _dot(w.astype(BF16), head_sum) * vs

    qt = (q * jnp.exp(b)).astype(BF16)
    kt = (k * jnp.exp(bl - b)).astype(BF16)
    dec = jnp.exp(bl)
    vb = v.astype(BF16)
    dh = W // HGRN_HEADS
    lane = lax.broadcasted_iota(jnp.int32, (1, W), 1)
    hmask = [(lane // dh) == hh for hh in range(HGRN_HEADS)]
    zero = jnp.zeros((), BF16)

    s = s_ref[...]
    order = range(T // C - 1, -1, -1) if reverse else range(T // C)
    for n in order:
        r0 = n * C
        qn = qt[r0:r0 + C]
        kn = kt[r0:r0 + C]
        vn = vb[r0:r0 + C]
        qexp = jnp.concatenate([jnp.where(hmask[hh], qn, zero) for hh in range(HGRN_HEADS)], axis=0)
        o4 = lax.dot_general(qexp, s.astype(BF16), (((1,), (1,)), ((), ())), preferred_element_type=F32)
        on = jnp.concatenate([o4[hh * C:(hh + 1) * C, :] for hh in range(HGRN_HEADS)], axis=1)
        o_ref[r0:r0 + C, :] = acc[r0:r0 + C] + on
        kexp = jnp.concatenate([jnp.where(hmask[hh], kn, zero) for hh in range(HGRN_HEADS)], axis=0)
        vexp = jnp.concatenate([vn[:, hh * dh:(hh + 1) * dh] for hh in range(HGRN_HEADS)], axis=0)
        u = lax.dot_general(vexp, kexp, (((0,), (0,)), ((), ())), preferred_element_type=F32)
        s = s * dec[r0:r0 + 1, :] + u
    s_ref[...] = s


def _hgrn(q, v, lf, k, *, T, reverse):
    B, L, W = q.shape
    nt = L // T
    idx = (lambda b, i: (b, nt - 1 - i, 0)) if reverse else (lambda b, i: (b, i, 0))
    spec = pl.BlockSpec((None, T, W), idx)
    return pl.pallas_call(
        functools.partial(_hgrn_kernel, T=T, reverse=reverse),
        grid=(B, nt),
        in_specs=[spec, spec, spec, spec],
        out_specs=spec,
        out_shape=jax.ShapeDtypeStruct((B, L, W), F32),
        scratch_shapes=[pltpu.VMEM((W // HGRN_HEADS, W), F32)],
        compiler_params=pltpu.CompilerParams(
            dimension_semantics=("parallel", "arbitrary"), vmem_limit_bytes=VMEM_LIMIT),
        name="hgrn_bwd" if reverse else "hgrn_fwd",
    )(q, v, lf, k)


def _out_mlp_kernel(x_ref, oa_ref, ofw_ref, obw_ref, sg_ref, hn_ref, up_ref, uc_ref, un_ref,
                    cw_ref, cb_ref, lg_ref, lb_ref, wo_ref, nm_ref, w1_ref, w2_ref, out_ref, ext_ref,
                    *, tm, ff_chunk):
    i = pl.program_id(1)
    nt = pl.num_programs(1)

    o = ofw_ref[...] + obw_ref[...]
    gmat = _group_mean_matrix(HGRN_WIDTH, HGRN_WIDTH // HGRN_HEADS)
    o_hg = o * lax.rsqrt(_split_dot(o * o, gmat, 2) + EPS) * hn_ref[...] * sg_ref[...]

    ext_ref[0:CONV_HALO, :] = jnp.where(i > 0, up_ref[...], 0.0)
    ext_ref[CONV_HALO:CONV_HALO + tm, :] = uc_ref[...]
    ext_ref[CONV_HALO + tm:, :] = jnp.where(i < nt - 1, un_ref[...], 0.0)
    y = jnp.zeros((tm, CONV_WIDTH), F32)
    for w in range(CONV_KERNEL):
        y = y + ext_ref[pl.ds(CONV_HALO - CONV_PAD + w, tm), :] * cw_ref[w:w + 1, :]
    y = y + cb_ref[...]
    mu = jnp.mean(y, axis=-1, keepdims=True)
    yc = y - mu
    yn = yc * lax.rsqrt(jnp.mean(yc * yc, axis=-1, keepdims=True) + EPS) * lg_ref[...] + lb_ref[...]
    o_cv = yn * _sigmoid(yn)

    mix = (_dot(oa_ref[...], wo_ref[0:ATT_WIDTH, :])
           + _dot(o_hg.astype(BF16), wo_ref[ATT_WIDTH:ATT_WIDTH + HGRN_WIDTH, :])
           + _dot(o_cv.astype(BF16), wo_ref[ATT_WIDTH + HGRN_WIDTH:, :]))
    x1 = x_ref[...] + mix

    ms = jnp.mean(x1 * x1, axis=-1, keepdims=True)
    hn = (x1 * lax.rsqrt(ms + EPS) * nm_ref[...]).astype(BF16)
    mlp = None
    d_ff = w1_ref.shape[1]
    for c0 in range(0, d_ff, ff_chunk):
        a = jnp.maximum(_dot(hn, w1_ref[:, c0:c0 + ff_chunk]), 0.0)
        part = _dot((a * a).astype(BF16), w2_ref[c0:c0 + ff_chunk, :])
        mlp = part if mlp is None else mlp + part
    out_ref[...] = x1 + mlp


def _out_mlp(x, oa, ofw, obw, sg, hn_g, u, cw, cb, lg, lb, wo, nm, w1, w2, *, tm):
    B, L, D = x.shape
    nt = L // tm
    hb = tm // CONV_HALO
    nhb = L // CONV_HALO
    tok = lambda width: pl.BlockSpec((None, tm, width), lambda b, i: (b, i, 0))
    const = lambda a: pl.BlockSpec(a.shape, lambda b, i: (0,) * a.ndim)
    in_specs = [
        tok(D), tok(ATT_WIDTH), tok(HGRN_WIDTH), tok(HGRN_WIDTH), tok(HGRN_WIDTH), const(hn_g),
        pl.BlockSpec((None, CONV_HALO, CONV_WIDTH), lambda b, i: (b, jnp.maximum(i * hb - 1, 0), 0)),
        tok(CONV_WIDTH),
        pl.BlockSpec((None, CONV_HALO, CONV_WIDTH), lambda b, i: (b, jnp.minimum((i + 1) * hb, nhb - 1), 0)),
        const(cw), const(cb), const(lg), const(lb), const(wo), const(nm), const(w1), const(w2),
    ]
    return pl.pallas_call(
        functools.partial(_out_mlp_kernel, tm=tm, ff_chunk=min(1024, w1.shape[1])),
        grid=(B, nt),
        in_specs=in_specs,
        out_specs=tok(D),
        out_shape=jax.ShapeDtypeStruct((B, L, D), F32),
        scratch_shapes=[pltpu.VMEM((tm + 2 * CONV_HALO, CONV_WIDTH), F32)],
        compiler_params=pltpu.CompilerParams(
            dimension_semantics=("parallel", "parallel"), vmem_limit_bytes=VMEM_LIMIT),
        name="out_mlp",
    )(x, oa, ofw, obw, sg, hn_g, u, u, u, cw, cb, lg, lb, wo, nm, w1, w2)


def _rope_tables(L):
    rows = L // GRID_W
    row_idx = jnp.repeat(jnp.arange(rows), GRID_W)
    col_idx = jnp.tile(jnp.arange(GRID_W), rows)
    inv_freq = ROPE_THETA ** (-jnp.arange(0, ROPE_AXIS_DIM, 2, dtype=F32) / ROPE_AXIS_DIM)
    ar = row_idx.astype(F32)[:, None] * inv_freq[None, :]
    ac = col_idx.astype(F32)[:, None] * inv_freq[None, :]
    cos = jnp.concatenate([jnp.cos(ar), jnp.cos(ar), jnp.cos(ac), jnp.cos(ac)], axis=-1)
    sin = jnp.concatenate([-jnp.sin(ar), jnp.sin(ar), -jnp.sin(ac), jnp.sin(ac)], axis=-1)
    reps = LANES // HEAD_DIM
    return jnp.tile(cos, (1, reps)), jnp.tile(sin, (1, reps))


def _tile(L, want):
    t = min(want, L)
    assert L % t == 0, (L, t)
    return t


def kernel(x, w_in, w_out, norm_mix, norm_mlp, q_norm, k_norm, hgrn_lb_fwd, hgrn_lb_bwd, hgrn_norm,
           conv_w, conv_b, conv_ln_g, conv_ln_b, w_mlp_in, w_mlp_out):
    B, L, D = x.shape
    depth = w_in.shape[0]
    assert w_in.shape[2] == IN_COLS and L % GRID_W == 0
    cos, sin = _rope_tables(L)
    row = lambda a: a.reshape(1, -1).astype(F32)
    reps = LANES // HEAD_DIM
    tm_in = _tile(L, 512)
    tq = _tile(L, 256)
    tk = _tile(L, 512)
    t_hg = _tile(L, 256)
    tm_out = _tile(L, 512)

    for l in range(depth):
        (q, kt, va, hq, hv, lff, kf, lfb, kb, sg, u) = _in_proj(
            x, row(norm_mix[l]), w_in[l].astype(BF16),
            jnp.tile(row(q_norm[l]), (1, reps)), jnp.tile(row(k_norm[l]), (1, reps)),
            cos, sin, hgrn_lb_fwd.astype(F32), hgrn_lb_bwd.astype(F32), layer=l, tm=tm_in)
        o_att = _attention(q, kt, va, tq=tq, tk=tk)
        o_fw = _hgrn(hq, hv, lff, kf, T=t_hg, reverse=False)
        o_bw = _hgrn(hq, hv, lfb, kb, T=t_hg, reverse=True)
        x = _out_mlp(
            x, o_att, o_fw, o_bw, sg, row(hgrn_norm[l]), u,
            conv_w[l].astype(F32), row(conv_b[l]), row(conv_ln_g[l]), row(conv_ln_b[l]),
            w_out[l].astype(BF16), row(norm_mlp[l]), w_mlp_in[l].astype(BF16), w_mlp_out[l].astype(BF16),
            tm=tm_out)
    return x
```

```python
import functools
import math

import jax
import jax.numpy as jnp
from jax import lax
from jax.experimental import pallas as pl
from jax.experimental.pallas import tpu as pltpu

F32 = jnp.float32
BF16 = jnp.bfloat16

EPS = 1e-6
GRID_W = 64
HEAD_DIM = 64
ATT_HEADS = 8
ATT_KV_HEADS = 2
ATT_GROUP = ATT_HEADS // ATT_KV_HEADS
ATT_WIDTH = ATT_HEADS * HEAD_DIM
KV_WIDTH = ATT_KV_HEADS * HEAD_DIM
ROPE_THETA = 10000.0
ROPE_AXIS_DIM = HEAD_DIM // 2
HGRN_WIDTH = 256
HGRN_HEADS = 4
HGRN_CHUNK = 16
CONV_WIDTH = 256
CONV_KERNEL = 31
CONV_PAD = (CONV_KERNEL - 1) // 2
CONV_HALO = 16

LANES = 128
LOG2E = 1.4426950408889634
VMEM_LIMIT = 56 * 1024 * 1024

_C_Q = 0
_C_K = _C_Q + ATT_WIDTH
_C_V = _C_K + KV_WIDTH
_C_HQ = _C_V + KV_WIDTH
_C_HI = _C_HQ + HGRN_WIDTH
_C_FF = _C_HI + HGRN_WIDTH
_C_FB = _C_FF + HGRN_WIDTH
_C_HG = _C_FB + HGRN_WIDTH
_C_CA = _C_HG + HGRN_WIDTH
_C_CG = _C_CA + CONV_WIDTH
IN_COLS = _C_CG + CONV_WIDTH


def _dot(a, b):
    return jnp.dot(a, b, preferred_element_type=F32)


def _split_dot(x, m, parts):
    acc = None
    r = x
    for _ in range(parts):
        t = r.astype(BF16)
        acc = _dot(t, m) if acc is None else acc + _dot(t, m)
        r = r - t.astype(F32)
    return acc


def _group_matrix(width, group, value):
    r = lax.broadcasted_iota(jnp.int32, (width, width), 0) // group
    c = lax.broadcasted_iota(jnp.int32, (width, width), 1) // group
    return jnp.where(r == c, value, 0.0).astype(BF16)


def _group_mean_matrix(width, group):
    return _group_matrix(width, group, 1.0 / group)


def _sigmoid(x):
    return 1.0 / (1.0 + jnp.exp(-x))


def _log_sigmoid(x):
    return jnp.minimum(x, 0.0) - jnp.log1p(jnp.exp(-jnp.abs(x)))


def _in_proj_kernel(x_ref, g_ref, w_ref, qn_ref, kn_ref, cos_ref, sin_ref, lbf_ref, lbb_ref,
                    q_out, kt_out, va_out, hq_out, hv_out, lff_out, kf_out, lfb_out, kb_out, sg_out, u_out,
                    *, layer):
    x = x_ref[...]
    ms = jnp.mean(x * x, axis=-1, keepdims=True)
    h = (x * lax.rsqrt(ms + EPS) * g_ref[...]).astype(BF16)

    def proj(lo, width):
        return _dot(h, w_ref[:, lo:lo + width])

    gmat = _group_mean_matrix(LANES, HEAD_DIM)
    cos = cos_ref[...]
    sin = sin_ref[...]
    lane = lax.broadcasted_iota(jnp.int32, (1, LANES), 1)
    first_half = (lane % ROPE_AXIS_DIM) < (ROPE_AXIS_DIM // 2)

    def norm_rope(z, gain, scale):
        zn = z * lax.rsqrt(_split_dot(z * z, gmat, 2) + EPS) * gain
        half = ROPE_AXIS_DIM // 2
        partner = jnp.where(first_half, pltpu.roll(zn, LANES - half, 1), pltpu.roll(zn, half, 1))
        return (zn * cos + partner * sin) * scale

    zq = proj(_C_Q, ATT_WIDTH)
    q_scale = HEAD_DIM ** -0.5 * LOG2E
    q_out[...] = jnp.concatenate(
        [norm_rope(zq[:, c * LANES:(c + 1) * LANES], qn_ref[...], q_scale) for c in range(ATT_WIDTH // LANES)],
        axis=1).astype(BF16)

    zk = norm_rope(proj(_C_K, KV_WIDTH), kn_ref[...], 1.0)
    zkt = jnp.transpose(zk).astype(BF16)
    for hh in range(ATT_KV_HEADS):
        kt_out[hh] = zkt[hh * HEAD_DIM:(hh + 1) * HEAD_DIM, :]

    zv = proj(_C_V, KV_WIDTH)
    low = lane < HEAD_DIM
    va_out[0] = jnp.where(low, zv, 1.0).astype(BF16)
    va_out[1] = jnp.where(low, pltpu.roll(zv, HEAD_DIM, 1), 1.0).astype(BF16)

    def lower_bound(lb_ref):
        p = lb_ref[...]
        p = jnp.exp(p - jnp.max(p, axis=0, keepdims=True))
        p = p / jnp.sum(p, axis=0, keepdims=True)
        lb = jnp.zeros((1, HGRN_WIDTH), F32)
        for i in range(1, layer + 1):
            lb = lb + p[i:i + 1, :]
        return lb

    def forget(z, lb):
        log_f = _log_sigmoid(z) + jnp.log1p(lb * jnp.exp(-z))
        return log_f, (1.0 - lb) * _sigmoid(-z)

    hq_out[...] = proj(_C_HQ, HGRN_WIDTH)
    hv_out[...] = proj(_C_HI, HGRN_WIDTH)
    lf, kk = forget(proj(_C_FF, HGRN_WIDTH), lower_bound(lbf_ref))
    lff_out[...] = lf
    kf_out[...] = kk
    lf, kk = forget(proj(_C_FB, HGRN_WIDTH), lower_bound(lbb_ref))
    lfb_out[...] = lf
    kb_out[...] = kk
    zg = proj(_C_HG, HGRN_WIDTH)
    sg_out[...] = zg * _sigmoid(zg)

    u_out[...] = proj(_C_CA, CONV_WIDTH) * _sigmoid(proj(_C_CG, CONV_WIDTH))


def _in_proj(x, g, w, qn, kn, cos, sin, lbf, lbb, *, layer, tm):
    B, L, D = x.shape
    nt = L // tm
    tok = lambda width: pl.BlockSpec((None, tm, width), lambda b, i: (b, i, 0))
    full = lambda a: pl.BlockSpec(a.shape, lambda b, i: (0,) * a.ndim)
    hg = jax.ShapeDtypeStruct((B, L, HGRN_WIDTH), F32)
    out_shape = (
        jax.ShapeDtypeStruct((B, L, ATT_WIDTH), BF16),
        jax.ShapeDtypeStruct((B, ATT_KV_HEADS, HEAD_DIM, L), BF16),
        jax.ShapeDtypeStruct((B, ATT_KV_HEADS, L, LANES), BF16),
        hg, hg, hg, hg, hg, hg, hg,
        jax.ShapeDtypeStruct((B, L, CONV_WIDTH), F32),
    )
    out_specs = (
        tok(ATT_WIDTH),
        pl.BlockSpec((None, ATT_KV_HEADS, HEAD_DIM, tm), lambda b, i: (b, 0, 0, i)),
        pl.BlockSpec((None, ATT_KV_HEADS, tm, LANES), lambda b, i: (b, 0, i, 0)),
        tok(HGRN_WIDTH), tok(HGRN_WIDTH), tok(HGRN_WIDTH), tok(HGRN_WIDTH), tok(HGRN_WIDTH), tok(HGRN_WIDTH),
        tok(HGRN_WIDTH), tok(CONV_WIDTH),
    )
    in_specs = [
        tok(D), full(g), full(w), full(qn), full(kn),
        pl.BlockSpec((tm, LANES), lambda b, i: (i, 0)),
        pl.BlockSpec((tm, LANES), lambda b, i: (i, 0)),
        full(lbf), full(lbb),
    ]
    return pl.pallas_call(
        functools.partial(_in_proj_kernel, layer=layer),
        grid=(B, nt),
        in_specs=in_specs,
        out_specs=out_specs,
        out_shape=out_shape,
        compiler_params=pltpu.CompilerParams(
            dimension_semantics=("parallel", "parallel"), vmem_limit_bytes=VMEM_LIMIT),
        name=f"in_proj_{layer}",
    )(x, g, w, qn, kn, cos, sin, lbf, lbb)


def _attn_kernel(q_ref, kt_ref, va_ref, o_ref, qs_ref, s_ref, m_ref, acc_ref, *, tq, tk, nk):
    q = q_ref[...]
    qs_ref[...] = jnp.concatenate(
        [q[:, g * HEAD_DIM:(g + 1) * HEAD_DIM] for g in range(ATT_GROUP)], axis=0)
    m_ref[...] = jnp.full(m_ref.shape, -jnp.inf, F32)
    acc_ref[...] = jnp.zeros(acc_ref.shape, F32)

    def scores(j, slot):
        k0 = pl.multiple_of(j * tk, tk)
        s_ref[slot] = _dot(qs_ref[...], kt_ref[:, pl.ds(k0, tk)])

    def softmax_pv(j, slot):
        k0 = pl.multiple_of(j * tk, tk)
        s = s_ref[slot]
        m_prev = m_ref[...]
        m_new = jnp.maximum(m_prev, jnp.max(s, axis=1, keepdims=True))
        p = jnp.exp2(s - pltpu.repeat(m_new, tk // LANES, 1)).astype(BF16)
        acc_ref[...] = jnp.exp2(m_prev - m_new) * acc_ref[...] + _dot(p, va_ref[pl.ds(k0, tk), :])
        m_ref[...] = m_new

    scores(0, 0)

    def body(jj, carry):
        j = 2 * jj
        scores(j + 1, 1)
        softmax_pv(j, 0)
        scores(j + 2, 0)
        softmax_pv(j + 1, 1)
        return carry

    lax.fori_loop(0, nk // 2 - 1, body, 0)
    scores(nk - 1, 1)
    softmax_pv(nk - 2, 0)
    softmax_pv(nk - 1, 1)
    acc = acc_ref[...]
    res = acc / pltpu.roll(acc, HEAD_DIM, 1)
    o_ref[...] = jnp.concatenate(
        [res[g * tq:(g + 1) * tq, :HEAD_DIM] for g in range(ATT_GROUP)], axis=1).astype(o_ref.dtype)


def _attention(q, kt, va, *, tq, tk):
    B, L, _ = q.shape
    gw = ATT_GROUP * HEAD_DIM
    assert (L // tk) % 2 == 0, "the kv loop handles two chunks per trip"
    return pl.pallas_call(
        functools.partial(_attn_kernel, tq=tq, tk=tk, nk=L // tk),
        grid=(B, ATT_KV_HEADS, L // tq),
        in_specs=[
            pl.BlockSpec((None, tq, gw), lambda b, h, i: (b, i, h)),
            pl.BlockSpec((None, None, HEAD_DIM, L), lambda b, h, i: (b, h, 0, 0)),
            pl.BlockSpec((None, None, L, LANES), lambda b, h, i: (b, h, 0, 0)),
        ],
        out_specs=pl.BlockSpec((None, tq, gw), lambda b, h, i: (b, i, h)),
        out_shape=jax.ShapeDtypeStruct((B, L, ATT_WIDTH), BF16),
        scratch_shapes=[
            pltpu.VMEM((ATT_GROUP * tq, HEAD_DIM), BF16),
            pltpu.VMEM((2, ATT_GROUP * tq, tk), F32),
            pltpu.VMEM((ATT_GROUP * tq, LANES), F32),
            pltpu.VMEM((ATT_GROUP * tq, LANES), F32),
        ],
        compiler_params=pltpu.CompilerParams(
            dimension_semantics=("parallel", "parallel", "parallel"), vmem_limit_bytes=VMEM_LIMIT),
        name="attention",
    )(q, kt, va)


def _hgrn_kernel(q_ref, v_ref, lf_ref, k_ref, o_ref, s_ref, *, T, reverse):
    C = HGRN_CHUNK
    W = HGRN_WIDTH

    @pl.when(pl.program_id(1) == 0)
    def _():
        s_ref[...] = jnp.zeros(s_ref.shape, F32)

    q = q_ref[...]
    v = v_ref[...]
    k = k_ref[...]
    lf = lf_ref[...]

    r = lax.broadcasted_iota(jnp.int32, (T, T), 0)
    c = lax.broadcasted_iota(jnp.int32, (T, T), 1)
    same = (r // C) == (c // C)
    tri = same & ((c >= r) if reverse else (c <= r))
    m_tri = jnp.where(tri, 1.0, 0.0).astype(BF16)
    m_all = jnp.where(same, 1.0, 0.0).astype(BF16)

    def left_split_dot(m, x):
        acc = None
        rem = x
        for _ in range(3):
            t = rem.astype(BF16)
            acc = _dot(m, t) if acc is None else acc + _dot(m, t)
            rem = rem - t.astype(F32)
        return acc

    b = left_split_dot(m_tri, lf)
    bl = left_split_dot(m_all, lf)

    pos = lax.broadcasted_iota(jnp.int32, (T, 1), 0) % C
    head_sum = _group_matrix(W, W // HGRN_HEADS, 1.0)
    acc = jnp.zeros((T, W), F32)
    for d in range(C):
        if d == 0:
            ks, bs, vs = k, b, v
        else:
            sh = (T - d) if reverse else d
            ks, bs, vs = (pltpu.roll(a, sh, 0) for a in (k, b, v))
        valid = (pos <= C - 1 - d) if reverse else (pos >= d)
        w = jnp.where(valid, q * ks * jnp.exp(b - bs), 0.0)
        acc = acc + _dot(w.astype(BF16), head_sum) * vs

    qt = (q * jnp.exp(b)).astype(BF16)
    kt = (k * jnp.exp(bl - b)).astype(BF16)
    dec = jnp.exp(bl)
    vb = v.astype(BF16)
    dh = W // HGRN_HEADS
    lane = lax.broadcasted_iota(jnp.int32, (1, W), 1)
    hmask = [(lane // dh) == hh for hh in range(HGRN_HEADS)]
    zero = jnp.zeros((), BF16)

    s = s_ref[...]
    order = range(T // C - 1, -1, -1) if reverse else range(T // C)
    for n in order:
        r0 = n * C
        qn = qt[r0:r0 + C]
        kn = kt[r0:r0 + C]
        vn = vb[r0:r0 + C]
        qexp = jnp.concatenate([jnp.where(hmask[hh], qn, zero) for hh in range(HGRN_HEADS)], axis=0)
        o4 = lax.dot_general(qexp, s.astype(BF16), (((1,), (1,)), ((), ())), preferred_element_type=F32)
        on = jnp.concatenate([o4[hh * C:(hh + 1) * C, :] for hh in range(HGRN_HEADS)], axis=1)
        o_ref[r0:r0 + C, :] = acc[r0:r0 + C] + on
        kexp = jnp.concatenate([jnp.where(hmask[hh], kn, zero) for hh in range(HGRN_HEADS)], axis=0)
        vexp = jnp.concatenate([vn[:, hh * dh:(hh + 1) * dh] for hh in range(HGRN_HEADS)], axis=0)
        u = lax.dot_general(vexp, kexp, (((0,), (0,)), ((), ())), preferred_element_type=F32)
        s = s * dec[r0:r0 + 1, :] + u
    s_ref[...] = s


def _hgrn(q, v, lf, k, *, T, reverse):
    B, L, W = q.shape
    nt = L // T
    idx = (lambda b, i: (b, nt - 1 - i, 0)) if reverse else (lambda b, i: (b, i, 0))
    spec = pl.BlockSpec((None, T, W), idx)
    return pl.pallas_call(
        functools.partial(_hgrn_kernel, T=T, reverse=reverse),
        grid=(B, nt),
        in_specs=[spec, spec, spec, spec],
        out_specs=spec,
        out_shape=jax.ShapeDtypeStruct((B, L, W), F32),
        scratch_shapes=[pltpu.VMEM((W // HGRN_HEADS, W), F32)],
        compiler_params=pltpu.CompilerParams(
            dimension_semantics=("parallel", "arbitrary"), vmem_limit_bytes=VMEM_LIMIT),
        name="hgrn_bwd" if reverse else "hgrn_fwd",
    )(q, v, lf, k)


def _out_mlp_kernel(x_ref, oa_ref, ofw_ref, obw_ref, sg_ref, hn_ref, up_ref, uc_ref, un_ref,
                    cw_ref, cb_ref, lg_ref, lb_ref, wo_ref, nm_ref, w1_ref, w2_ref, out_ref, ext_ref,
                    *, tm, ff_chunk):
    i = pl.program_id(1)
    nt = pl.num_programs(1)

    o = ofw_ref[...] + obw_ref[...]
    gmat = _group_mean_matrix(HGRN_WIDTH, HGRN_WIDTH // HGRN_HEADS)
    o_hg = o * lax.rsqrt(_split_dot(o * o, gmat, 2) + EPS) * hn_ref[...] * sg_ref[...]

    ext_ref[0:CONV_HALO, :] = jnp.where(i > 0, up_ref[...], 0.0)
    ext_ref[CONV_HALO:CONV_HALO + tm, :] = uc_ref[...]
    ext_ref[CONV_HALO + tm:, :] = jnp.where(i < nt - 1, un_ref[...], 0.0)
    y = jnp.zeros((tm, CONV_WIDTH), F32)
    for w in range(CONV_KERNEL):
        y = y + ext_ref[pl.ds(CONV_HALO - CONV_PAD + w, tm), :] * cw_ref[w:w + 1, :]
    y = y + cb_ref[...]
    mu = jnp.mean(y, axis=-1, keepdims=True)
    yc = y - mu
    yn = yc * lax.rsqrt(jnp.mean(yc * yc, axis=-1, keepdims=True) + EPS) * lg_ref[...] + lb_ref[...]
    o_cv = yn * _sigmoid(yn)

    mix = (_dot(oa_ref[...], wo_ref[0:ATT_WIDTH, :])
           + _dot(o_hg.astype(BF16), wo_ref[ATT_WIDTH:ATT_WIDTH + HGRN_WIDTH, :])
           + _dot(o_cv.astype(BF16), wo_ref[ATT_WIDTH + HGRN_WIDTH:, :]))
    x1 = x_ref[...] + mix

    ms = jnp.mean(x1 * x1, axis=-1, keepdims=True)
    hn = (x1 * lax.rsqrt(ms + EPS) * nm_ref[...]).astype(BF16)
    mlp = None
    d_ff = w1_ref.shape[1]
    for c0 in range(0, d_ff, ff_chunk):
        a = jnp.maximum(_dot(hn, w1_ref[:, c0:c0 + ff_chunk]), 0.0)
        part = _dot((a * a).astype(BF16), w2_ref[c0:c0 + ff_chunk, :])
        mlp = part if mlp is None else mlp + part
    out_ref[...] = x1 + mlp


def _out_mlp(x, oa, ofw, obw, sg, hn_g, u, cw, cb, lg, lb, wo, nm, w1, w2, *, tm):
    B, L, D = x.shape
    nt = L // tm
    hb = tm // CONV_HALO
    nhb = L // CONV_HALO
    tok = lambda width: pl.BlockSpec((None, tm, width), lambda b, i: (b, i, 0))
    const = lambda a: pl.BlockSpec(a.shape, lambda b, i: (0,) * a.ndim)
    in_specs = [
        tok(D), tok(ATT_WIDTH), tok(HGRN_WIDTH), tok(HGRN_WIDTH), tok(HGRN_WIDTH), const(hn_g),
        pl.BlockSpec((None, CONV_HALO, CONV_WIDTH), lambda b, i: (b, jnp.maximum(i * hb - 1, 0), 0)),
        tok(CONV_WIDTH),
        pl.BlockSpec((None, CONV_HALO, CONV_WIDTH), lambda b, i: (b, jnp.minimum((i + 1) * hb, nhb - 1), 0)),
        const(cw), const(cb), const(lg), const(lb), const(wo), const(nm), const(w1), const(w2),
    ]
    return pl.pallas_call(
        functools.partial(_out_mlp_kernel, tm=tm, ff_chunk=min(1024, w1.shape[1])),
        grid=(B, nt),
        in_specs=in_specs,
        out_specs=tok(D),
        out_shape=jax.ShapeDtypeStruct((B, L, D), F32),
        scratch_shapes=[pltpu.VMEM((tm + 2 * CONV_HALO, CONV_WIDTH), F32)],
        compiler_params=pltpu.CompilerParams(
            dimension_semantics=("parallel", "parallel"), vmem_limit_bytes=VMEM_LIMIT),
        name="out_mlp",
    )(x, oa, ofw, obw, sg, hn_g, u, u, u, cw, cb, lg, lb, wo, nm, w1, w2)


def _rope_tables(L):
    rows = L // GRID_W
    row_idx = jnp.repeat(jnp.arange(rows), GRID_W)
    col_idx = jnp.tile(jnp.arange(GRID_W), rows)
    inv_freq = ROPE_THETA ** (-jnp.arange(0, ROPE_AXIS_DIM, 2, dtype=F32) / ROPE_AXIS_DIM)
    ar = row_idx.astype(F32)[:, None] * inv_freq[None, :]
    ac = col_idx.astype(F32)[:, None] * inv_freq[None, :]
    cos = jnp.concatenate([jnp.cos(ar), jnp.cos(ar), jnp.cos(ac), jnp.cos(ac)], axis=-1)
    sin = jnp.concatenate([-jnp.sin(ar), jnp.sin(ar), -jnp.sin(ac), jnp.sin(ac)], axis=-1)
    reps = LANES // HEAD_DIM
    return jnp.tile(cos, (1, reps)), jnp.tile(sin, (1, reps))


def _tile(L, want):
    t = min(want, L)
    assert L % t == 0, (L, t)
    return t


def kernel(x, w_in, w_out, norm_mix, norm_mlp, q_norm, k_norm, hgrn_lb_fwd, hgrn_lb_bwd, hgrn_norm,
           conv_w, conv_b, conv_ln_g, conv_ln_b, w_mlp_in, w_mlp_out):
    B, L, D = x.shape
    depth = w_in.shape[0]
    assert w_in.shape[2] == IN_COLS and L % GRID_W == 0
    cos, sin = _rope_tables(L)
    row = lambda a: a.reshape(1, -1).astype(F32)
    reps = LANES // HEAD_DIM
    tm_in = _tile(L, 512)
    tq = _tile(L, 512)
    tk = _tile(L // 2, 512)
    t_hg = _tile(L, 256)
    tm_out = _tile(L, 512)

    for l in range(depth):
        (q, kt, va, hq, hv, lff, kf, lfb, kb, sg, u) = _in_proj(
            x, row(norm_mix[l]), w_in[l].astype(BF16),
            jnp.tile(row(q_norm[l]), (1, reps)), jnp.tile(row(k_norm[l]), (1, reps)),
            cos, sin, hgrn_lb_fwd.astype(F32), hgrn_lb_bwd.astype(F32), layer=l, tm=tm_in)
        o_att = _attention(q, kt, va, tq=tq, tk=tk)
        o_fw = _hgrn(hq, hv, lff, kf, T=t_hg, reverse=False)
        o_bw = _hgrn(hq, hv, lfb, kb, T=t_hg, reverse=True)
        x = _out_mlp(
            x, o_att, o_fw, o_bw, sg, row(hgrn_norm[l]), u,
            conv_w[l].astype(F32), row(conv_b[l]), row(conv_ln_g[l]), row(conv_ln_b[l]),
            w_out[l].astype(BF16), row(norm_mlp[l]), w_mlp_in[l].astype(BF16), w_mlp_out[l].astype(BF16),
            tm=tm_out)
    return x
```

```python
import functools
import math

import jax
import jax.numpy as jnp
from jax import lax
from jax.experimental import pallas as pl
from jax.experimental.pallas import tpu as pltpu

F32 = jnp.float32
BF16 = jnp.bfloat16

EPS = 1e-6
GRID_W = 64
HEAD_DIM = 64
ATT_HEADS = 8
ATT_KV_HEADS = 2
ATT_GROUP = ATT_HEADS // ATT_KV_HEADS
ATT_WIDTH = ATT_HEADS * HEAD_DIM
KV_WIDTH = ATT_KV_HEADS * HEAD_DIM
ROPE_THETA = 10000.0
ROPE_AXIS_DIM = HEAD_DIM // 2
HGRN_WIDTH = 256
HGRN_HEADS = 4
HGRN_CHUNK = 16
CONV_WIDTH = 256
CONV_KERNEL = 31
CONV_PAD = (CONV_KERNEL - 1) // 2
CONV_HALO = 16

LANES = 128
LOG2E = 1.4426950408889634
VMEM_LIMIT = 56 * 1024 * 1024

_C_Q = 0
_C_K = _C_Q + ATT_WIDTH
_C_V = _C_K + KV_WIDTH
_C_HQ = _C_V + KV_WIDTH
_C_HI = _C_HQ + HGRN_WIDTH
_C_FF = _C_HI + HGRN_WIDTH
_C_FB = _C_FF + HGRN_WIDTH
_C_HG = _C_FB + HGRN_WIDTH
_C_CA = _C_HG + HGRN_WIDTH
_C_CG = _C_CA + CONV_WIDTH
IN_COLS = _C_CG + CONV_WIDTH


def _dot(a, b):
    return jnp.dot(a, b, preferred_element_type=F32)


def _split_dot(x, m, parts):
    acc = None
    r = x
    for _ in range(parts):
        t = r.astype(BF16)
        acc = _dot(t, m) if acc is None else acc + _dot(t, m)
        r = r - t.astype(F32)
    return acc


def _group_matrix(width, group, value):
    r = lax.broadcasted_iota(jnp.int32, (width, width), 0) // group
    c = lax.broadcasted_iota(jnp.int32, (width, width), 1) // group
    return jnp.where(r == c, value, 0.0).astype(BF16)


def _group_mean_matrix(width, group):
    return _group_matrix(width, group, 1.0 / group)


def _sigmoid(x):
    return 1.0 / (1.0 + jnp.exp(-x))


def _log_sigmoid(x):
    return jnp.minimum(x, 0.0) - jnp.log1p(jnp.exp(-jnp.abs(x)))


def _in_proj_kernel(x_ref, g_ref, w_ref, qn_ref, kn_ref, cos_ref, sin_ref, lbf_ref, lbb_ref,
                    q_out, kt_out, va_out, hq_out, hv_out, lff_out, kf_out, lfb_out, kb_out, sg_out, u_out,
                    *, layer):
    x = x_ref[...]
    ms = jnp.mean(x * x, axis=-1, keepdims=True)
    h = (x * lax.rsqrt(ms + EPS) * g_ref[...]).astype(BF16)

    def proj(lo, width):
        return _dot(h, w_ref[:, lo:lo + width])

    gmat = _group_mean_matrix(LANES, HEAD_DIM)
    cos = cos_ref[...]
    sin = sin_ref[...]
    lane = lax.broadcasted_iota(jnp.int32, (1, LANES), 1)
    first_half = (lane % ROPE_AXIS_DIM) < (ROPE_AXIS_DIM // 2)

    def norm_rope(z, gain, scale):
        zn = z * lax.rsqrt(_split_dot(z * z, gmat, 2) + EPS) * gain
        half = ROPE_AXIS_DIM // 2
        partner = jnp.where(first_half, pltpu.roll(zn, LANES - half, 1), pltpu.roll(zn, half, 1))
        return (zn * cos + partner * sin) * scale

    zq = proj(_C_Q, ATT_WIDTH)
    q_scale = HEAD_DIM ** -0.5 * LOG2E
    q_out[...] = jnp.concatenate(
        [norm_rope(zq[:, c * LANES:(c + 1) * LANES], qn_ref[...], q_scale) for c in range(ATT_WIDTH // LANES)],
        axis=1).astype(BF16)

    zk = norm_rope(proj(_C_K, KV_WIDTH), kn_ref[...], 1.0)
    zkt = jnp.transpose(zk).astype(BF16)
    for hh in range(ATT_KV_HEADS):
        kt_out[hh] = zkt[hh * HEAD_DIM:(hh + 1) * HEAD_DIM, :]

    zv = proj(_C_V, KV_WIDTH)
    low = lane < HEAD_DIM
    va_out[0] = jnp.where(low, zv, 1.0).astype(BF16)
    va_out[1] = jnp.where(low, pltpu.roll(zv, HEAD_DIM, 1), 1.0).astype(BF16)

    def lower_bound(lb_ref):
        p = lb_ref[...]
        p = jnp.exp(p - jnp.max(p, axis=0, keepdims=True))
        p = p / jnp.sum(p, axis=0, keepdims=True)
        lb = jnp.zeros((1, HGRN_WIDTH), F32)
        for i in range(1, layer + 1):
            lb = lb + p[i:i + 1, :]
        return lb

    def forget(z, lb):
        log_f = _log_sigmoid(z) + jnp.log1p(lb * jnp.exp(-z))
        return log_f, (1.0 - lb) * _sigmoid(-z)

    hq_out[...] = proj(_C_HQ, HGRN_WIDTH)
    hv_out[...] = proj(_C_HI, HGRN_WIDTH)
    lf, kk = forget(proj(_C_FF, HGRN_WIDTH), lower_bound(lbf_ref))
    lff_out[...] = lf
    kf_out[...] = kk
    lf, kk = forget(proj(_C_FB, HGRN_WIDTH), lower_bound(lbb_ref))
    lfb_out[...] = lf
    kb_out[...] = kk
    zg = proj(_C_HG, HGRN_WIDTH)
    sg_out[...] = zg * _sigmoid(zg)

    u_out[...] = proj(_C_CA, CONV_WIDTH) * _sigmoid(proj(_C_CG, CONV_WIDTH))


def _in_proj(x, g, w, qn, kn, cos, sin, lbf, lbb, *, layer, tm):
    B, L, D = x.shape
    nt = L // tm
    tok = lambda width: pl.BlockSpec((None, tm, width), lambda b, i: (b, i, 0))
    full = lambda a: pl.BlockSpec(a.shape, lambda b, i: (0,) * a.ndim)
    hg = jax.ShapeDtypeStruct((B, L, HGRN_WIDTH), F32)
    out_shape = (
        jax.ShapeDtypeStruct((B, L, ATT_WIDTH), BF16),
        jax.ShapeDtypeStruct((B, ATT_KV_HEADS, HEAD_DIM, L), BF16),
        jax.ShapeDtypeStruct((B, ATT_KV_HEADS, L, LANES), BF16),
        hg, hg, hg, hg, hg, hg, hg,
        jax.ShapeDtypeStruct((B, L, CONV_WIDTH), F32),
    )
    out_specs = (
        tok(ATT_WIDTH),
        pl.BlockSpec((None, ATT_KV_HEADS, HEAD_DIM, tm), lambda b, i: (b, 0, 0, i)),
        pl.BlockSpec((None, ATT_KV_HEADS, tm, LANES), lambda b, i: (b, 0, i, 0)),
        tok(HGRN_WIDTH), tok(HGRN_WIDTH), tok(HGRN_WIDTH), tok(HGRN_WIDTH), tok(HGRN_WIDTH), tok(HGRN_WIDTH),
        tok(HGRN_WIDTH), tok(CONV_WIDTH),
    )
    in_specs = [
        tok(D), full(g), full(w), full(qn), full(kn),
        pl.BlockSpec((tm, LANES), lambda b, i: (i, 0)),
        pl.BlockSpec((tm, LANES), lambda b, i: (i, 0)),
        full(lbf), full(lbb),
    ]
    return pl.pallas_call(
        functools.partial(_in_proj_kernel, layer=layer),
        grid=(B, nt),
        in_specs=in_specs,
        out_specs=out_specs,
        out_shape=out_shape,
        compiler_params=pltpu.CompilerParams(
            dimension_semantics=("parallel", "parallel"), vmem_limit_bytes=VMEM_LIMIT),
        name=f"in_proj_{layer}",
    )(x, g, w, qn, kn, cos, sin, lbf, lbb)


def _attn_kernel(q_ref, kt_ref, va_ref, o_ref, qs_ref, s_ref, smax_ref, m_ref, acc_ref, *, tq, tk, nk):
    q = q_ref[...]
    qs_ref[...] = jnp.concatenate(
        [q[:, g * HEAD_DIM:(g + 1) * HEAD_DIM] for g in range(ATT_GROUP)], axis=0)
    m_ref[...] = jnp.full(m_ref.shape, -jnp.inf, F32)
    acc_ref[...] = jnp.zeros(acc_ref.shape, F32)

    def scores(j, slot):
        k0 = pl.multiple_of(j * tk, tk)
        s = _dot(qs_ref[...], kt_ref[:, pl.ds(k0, tk)])
        s_ref[slot] = s
        smax_ref[slot] = jnp.broadcast_to(jnp.max(s, axis=1, keepdims=True), smax_ref.shape[1:])

    def softmax_pv(j, slot):
        k0 = pl.multiple_of(j * tk, tk)
        s = s_ref[slot]
        m_prev = m_ref[...]
        m_new = jnp.maximum(m_prev, smax_ref[slot])
        p = jnp.exp2(s - jnp.tile(m_new, (1, tk // LANES))).astype(BF16)
        acc_ref[...] = jnp.exp2(m_prev - m_new) * acc_ref[...] + _dot(p, va_ref[pl.ds(k0, tk), :])
        m_ref[...] = m_new

    scores(0, 0)

    def body(jj, carry):
        j = 2 * jj
        scores(j + 1, 1)
        softmax_pv(j, 0)
        scores(j + 2, 0)
        softmax_pv(j + 1, 1)
        return carry

    lax.fori_loop(0, nk // 2 - 1, body, 0)
    scores(nk - 1, 1)
    softmax_pv(nk - 2, 0)
    softmax_pv(nk - 1, 1)
    acc = acc_ref[...]
    res = acc / pltpu.roll(acc, HEAD_DIM, 1)
    o_ref[...] = jnp.concatenate(
        [res[g * tq:(g + 1) * tq, :HEAD_DIM] for g in range(ATT_GROUP)], axis=1).astype(o_ref.dtype)


def _attention(q, kt, va, *, tq, tk):
    B, L, _ = q.shape
    gw = ATT_GROUP * HEAD_DIM
    assert (L // tk) % 2 == 0, "the kv loop handles two chunks per trip"
    return pl.pallas_call(
        functools.partial(_attn_kernel, tq=tq, tk=tk, nk=L // tk),
        grid=(B, ATT_KV_HEADS, L // tq),
        in_specs=[
            pl.BlockSpec((None, tq, gw), lambda b, h, i: (b, i, h)),
            pl.BlockSpec((None, None, HEAD_DIM, L), lambda b, h, i: (b, h, 0, 0)),
            pl.BlockSpec((None, None, L, LANES), lambda b, h, i: (b, h, 0, 0)),
        ],
        out_specs=pl.BlockSpec((None, tq, gw), lambda b, h, i: (b, i, h)),
        out_shape=jax.ShapeDtypeStruct((B, L, ATT_WIDTH), BF16),
        scratch_shapes=[
            pltpu.VMEM((ATT_GROUP * tq, HEAD_DIM), BF16),
            pltpu.VMEM((2, ATT_GROUP * tq, tk), F32),
            pltpu.VMEM((2, ATT_GROUP * tq, LANES), F32),
            pltpu.VMEM((ATT_GROUP * tq, LANES), F32),
            pltpu.VMEM((ATT_GROUP * tq, LANES), F32),
        ],
        compiler_params=pltpu.CompilerParams(
            dimension_semantics=("parallel", "parallel", "parallel"), vmem_limit_bytes=VMEM_LIMIT),
        name="attention",
    )(q, kt, va)


def _hgrn_kernel(q_ref, v_ref, lf_ref, k_ref, o_ref, s_ref, *, T, reverse):
    C = HGRN_CHUNK
    W = HGRN_WIDTH

    @pl.when(pl.program_id(1) == 0)
    def _():
        s_ref[...] = jnp.zeros(s_ref.shape, F32)

    q = q_ref[...]
    v = v_ref[...]
    k = k_ref[...]
    lf = lf_ref[...]

    r = lax.broadcasted_iota(jnp.int32, (T, T), 0)
    c = lax.broadcasted_iota(jnp.int32, (T, T), 1)
    same = (r // C) == (c // C)
    tri = same & ((c >= r) if reverse else (c <= r))
    m_tri = jnp.where(tri, 1.0, 0.0).astype(BF16)
    m_all = jnp.where(same, 1.0, 0.0).astype(BF16)

    def left_split_dot(m, x):
        acc = None
        rem = x
        for _ in range(3):
            t = rem.astype(BF16)
            acc = _dot(m, t) if acc is None else acc + _dot(m, t)
            rem = rem - t.astype(F32)
        return acc

    b = left_split_dot(m_tri, lf)
    bl = left_split_dot(m_all, lf)

    pos = lax.broadcasted_iota(jnp.int32, (T, 1), 0) % C
    head_sum = _group_matrix(W, W // HGRN_HEADS, 1.0)
    acc = jnp.zeros((T, W), F32)
    for d in range(C):
        if d == 0:
            ks, bs, vs = k, b, v
        else:
            sh = (T - d) if reverse else d
            ks, bs, vs = (pltpu.roll(a, sh, 0) for a in (k, b, v))
        valid = (pos <= C - 1 - d) if reverse else (pos >= d)
        w = jnp.where(valid, q * ks * jnp.exp(b - bs), 0.0)
        acc = acc + _dot(w.astype(BF16), head_sum) * vs

    qt = (q * jnp.exp(b)).astype(BF16)
    kt = (k * jnp.exp(bl - b)).astype(BF16)
    dec = jnp.exp(bl)
    vb = v.astype(BF16)
    dh = W // HGRN_HEADS
    lane = lax.broadcasted_iota(jnp.int32, (1, W), 1)
    hmask = [(lane // dh) == hh for hh in range(HGRN_HEADS)]
    zero = jnp.zeros((), BF16)

    s = s_ref[...]
    order = range(T // C - 1, -1, -1) if reverse else range(T // C)
    for n in order:
        r0 = n * C
        qn = qt[r0:r0 + C]
        kn = kt[r0:r0 + C]
        vn = vb[r0:r0 + C]
        qexp = jnp.concatenate([jnp.where(hmask[hh], qn, zero) for hh in range(HGRN_HEADS)], axis=0)
        o4 = lax.dot_general(qexp, s.astype(BF16), (((1,), (1,)), ((), ())), preferred_element_type=F32)
        on = jnp.concatenate([o4[hh * C:(hh + 1) * C, :] for hh in range(HGRN_HEADS)], axis=1)
        o_ref[r0:r0 + C, :] = acc[r0:r0 + C] + on
        kexp = jnp.concatenate([jnp.where(hmask[hh], kn, zero) for hh in range(HGRN_HEADS)], axis=0)
        vexp = jnp.concatenate([vn[:, hh * dh:(hh + 1) * dh] for hh in range(HGRN_HEADS)], axis=0)
        u = lax.dot_general(vexp, kexp, (((0,), (0,)), ((), ())), preferred_element_type=F32)
        s = s * dec[r0:r0 + 1, :] + u
    s_ref[...] = s


def _hgrn(q, v, lf, k, *, T, reverse):
    B, L, W = q.shape
    nt = L // T
    idx = (lambda b, i: (b, nt - 1 - i, 0)) if reverse else (lambda b, i: (b, i, 0))
    spec = pl.BlockSpec((None, T, W), idx)
    return pl.pallas_call(
        functools.partial(_hgrn_kernel, T=T, reverse=reverse),
        grid=(B, nt),
        in_specs=[spec, spec, spec, spec],
        out_specs=spec,
        out_shape=jax.ShapeDtypeStruct((B, L, W), F32),
        scratch_shapes=[pltpu.VMEM((W // HGRN_HEADS, W), F32)],
        compiler_params=pltpu.CompilerParams(
            dimension_semantics=("parallel", "arbitrary"), vmem_limit_bytes=VMEM_LIMIT),
        name="hgrn_bwd" if reverse else "hgrn_fwd",
    )(q, v, lf, k)


def _out_mlp_kernel(x_ref, oa_ref, ofw_ref, obw_ref, sg_ref, hn_ref, up_ref, uc_ref, un_ref,
                    cw_ref, cb_ref, lg_ref, lb_ref, wo_ref, nm_ref, w1_ref, w2_ref, out_ref, ext_ref,
                    *, tm, ff_chunk):
    i = pl.program_id(1)
    nt = pl.num_programs(1)

    o = ofw_ref[...] + obw_ref[...]
    gmat = _group_mean_matrix(HGRN_WIDTH, HGRN_WIDTH // HGRN_HEADS)
    o_hg = o * lax.rsqrt(_split_dot(o * o, gmat, 2) + EPS) * hn_ref[...] * sg_ref[...]

    ext_ref[0:CONV_HALO, :] = jnp.where(i > 0, up_ref[...], 0.0)
    ext_ref[CONV_HALO:CONV_HALO + tm, :] = uc_ref[...]
    ext_ref[CONV_HALO + tm:, :] = jnp.where(i < nt - 1, un_ref[...], 0.0)
    y = jnp.zeros((tm, CONV_WIDTH), F32)
    for w in range(CONV_KERNEL):
        y = y + ext_ref[pl.ds(CONV_HALO - CONV_PAD + w, tm), :] * cw_ref[w:w + 1, :]
    y = y + cb_ref[...]
    mu = jnp.mean(y, axis=-1, keepdims=True)
    yc = y - mu
    yn = yc * lax.rsqrt(jnp.mean(yc * yc, axis=-1, keepdims=True) + EPS) * lg_ref[...] + lb_ref[...]
    o_cv = yn * _sigmoid(yn)

    mix = (_dot(oa_ref[...], wo_ref[0:ATT_WIDTH, :])
           + _dot(o_hg.astype(BF16), wo_ref[ATT_WIDTH:ATT_WIDTH + HGRN_WIDTH, :])
           + _dot(o_cv.astype(BF16), wo_ref[ATT_WIDTH + HGRN_WIDTH:, :]))
    x1 = x_ref[...] + mix

    ms = jnp.mean(x1 * x1, axis=-1, keepdims=True)
    hn = (x1 * lax.rsqrt(ms + EPS) * nm_ref[...]).astype(BF16)
    mlp = None
    d_ff = w1_ref.shape[1]
    for c0 in range(0, d_ff, ff_chunk):
        a = jnp.maximum(_dot(hn, w1_ref[:, c0:c0 + ff_chunk]), 0.0)
        part = _dot((a * a).astype(BF16), w2_ref[c0:c0 + ff_chunk, :])
        mlp = part if mlp is None else mlp + part
    out_ref[...] = x1 + mlp


def _out_mlp(x, oa, ofw, obw, sg, hn_g, u, cw, cb, lg, lb, wo, nm, w1, w2, *, tm):
    B, L, D = x.shape
    nt = L // tm
    hb = tm // CONV_HALO
    nhb = L // CONV_HALO
    tok = lambda width: pl.BlockSpec((None, tm, width), lambda b, i: (b, i, 0))
    const = lambda a: pl.BlockSpec(a.shape, lambda b, i: (0,) * a.ndim)
    in_specs = [
        tok(D), tok(ATT_WIDTH), tok(HGRN_WIDTH), tok(HGRN_WIDTH), tok(HGRN_WIDTH), const(hn_g),
        pl.BlockSpec((None, CONV_HALO, CONV_WIDTH), lambda b, i: (b, jnp.maximum(i * hb - 1, 0), 0)),
        tok(CONV_WIDTH),
        pl.BlockSpec((None, CONV_HALO, CONV_WIDTH), lambda b, i: (b, jnp.minimum((i + 1) * hb, nhb - 1), 0)),
        const(cw), const(cb), const(lg), const(lb), const(wo), const(nm), const(w1), const(w2),
    ]
    return pl.pallas_call(
        functools.partial(_out_mlp_kernel, tm=tm, ff_chunk=min(1024, w1.shape[1])),
        grid=(B, nt),
        in_specs=in_specs,
        out_specs=tok(D),
        out_shape=jax.ShapeDtypeStruct((B, L, D), F32),
        scratch_shapes=[pltpu.VMEM((tm + 2 * CONV_HALO, CONV_WIDTH), F32)],
        compiler_params=pltpu.CompilerParams(
            dimension_semantics=("parallel", "parallel"), vmem_limit_bytes=VMEM_LIMIT),
        name="out_mlp",
    )(x, oa, ofw, obw, sg, hn_g, u, u, u, cw, cb, lg, lb, wo, nm, w1, w2)


def _rope_tables(L):
    rows = L // GRID_W
    row_idx = jnp.repeat(jnp.arange(rows), GRID_W)
    col_idx = jnp.tile(jnp.arange(GRID_W), rows)
    inv_freq = ROPE_THETA ** (-jnp.arange(0, ROPE_AXIS_DIM, 2, dtype=F32) / ROPE_AXIS_DIM)
    ar = row_idx.astype(F32)[:, None] * inv_freq[None, :]
    ac = col_idx.astype(F32)[:, None] * inv_freq[None, :]
    cos = jnp.concatenate([jnp.cos(ar), jnp.cos(ar), jnp.cos(ac), jnp.cos(ac)], axis=-1)
    sin = jnp.concatenate([-jnp.sin(ar), jnp.sin(ar), -jnp.sin(ac), jnp.sin(ac)], axis=-1)
    reps = LANES // HEAD_DIM
    return jnp.tile(cos, (1, reps)), jnp.tile(sin, (1, reps))


def _tile(L, want):
    t = min(want, L)
    assert L % t == 0, (L, t)
    return t


def kernel(x, w_in, w_out, norm_mix, norm_mlp, q_norm, k_norm, hgrn_lb_fwd, hgrn_lb_bwd, hgrn_norm,
           conv_w, conv_b, conv_ln_g, conv_ln_b, w_mlp_in, w_mlp_out):
    B, L, D = x.shape
    depth = w_in.shape[0]
    assert w_in.shape[2] == IN_COLS and L % GRID_W == 0
    cos, sin = _rope_tables(L)
    row = lambda a: a.reshape(1, -1).astype(F32)
    reps = LANES // HEAD_DIM
    tm_in = _tile(L, 512)
    tq = _tile(L, 512)
    tk = _tile(L // 2, 512)
    t_hg = _tile(L, 256)
    tm_out = _tile(L, 512)

    for l in range(depth):
        (q, kt, va, hq, hv, lff, kf, lfb, kb, sg, u) = _in_proj(
            x, row(norm_mix[l]), w_in[l].astype(BF16),
            jnp.tile(row(q_norm[l]), (1, reps)), jnp.tile(row(k_norm[l]), (1, reps)),
            cos, sin, hgrn_lb_fwd.astype(F32), hgrn_lb_bwd.astype(F32), layer=l, tm=tm_in)
        o_att = _attention(q, kt, va, tq=tq, tk=tk)
        o_fw = _hgrn(hq, hv, lff, kf, T=t_hg, reverse=False)
        o_bw = _hgrn(hq, hv, lfb, kb, T=t_hg, reverse=True)
        x = _out_mlp(
            x, o_att, o_fw, o_bw, sg, row(hgrn_norm[l]), u,
            conv_w[l].astype(F32), row(conv_b[l]), row(conv_ln_g[l]), row(conv_ln_b[l]),
            w_out[l].astype(BF16), row(norm_mlp[l]), w_mlp_in[l].astype(BF16), w_mlp_out[l].astype(BF16),
            tm=tm_out)
    return x
```

```python
import functools
import math

import jax
import jax.numpy as jnp
from jax import lax
from jax.experimental import pallas as pl
from jax.experimental.pallas import tpu as pltpu

F32 = jnp.float32
BF16 = jnp.bfloat16

EPS = 1e-6
GRID_W = 64
HEAD_DIM = 64
ATT_HEADS = 8
ATT_KV_HEADS = 2
ATT_GROUP = ATT_HEADS // ATT_KV_HEADS
ATT_WIDTH = ATT_HEADS * HEAD_DIM
KV_WIDTH = ATT_KV_HEADS * HEAD_DIM
ROPE_THETA = 10000.0
ROPE_AXIS_DIM = HEAD_DIM // 2
HGRN_WIDTH = 256
HGRN_HEADS = 4
HGRN_CHUNK = 16
CONV_WIDTH = 256
CONV_KERNEL = 31
CONV_PAD = (CONV_KERNEL - 1) // 2
CONV_HALO = 16

LANES = 128
LOG2E = 1.4426950408889634
VMEM_LIMIT = 56 * 1024 * 1024

_C_Q = 0
_C_K = _C_Q + ATT_WIDTH
_C_V = _C_K + KV_WIDTH
_C_HQ = _C_V + KV_WIDTH
_C_HI = _C_HQ + HGRN_WIDTH
_C_FF = _C_HI + HGRN_WIDTH
_C_FB = _C_FF + HGRN_WIDTH
_C_HG = _C_FB + HGRN_WIDTH
_C_CA = _C_HG + HGRN_WIDTH
_C_CG = _C_CA + CONV_WIDTH
IN_COLS = _C_CG + CONV_WIDTH


def _dot(a, b):
    return jnp.dot(a, b, preferred_element_type=F32)


def _split_dot(x, m, parts):
    acc = None
    r = x
    for _ in range(parts):
        t = r.astype(BF16)
        acc = _dot(t, m) if acc is None else acc + _dot(t, m)
        r = r - t.astype(F32)
    return acc


def _group_matrix(width, group, value):
    r = lax.broadcasted_iota(jnp.int32, (width, width), 0) // group
    c = lax.broadcasted_iota(jnp.int32, (width, width), 1) // group
    return jnp.where(r == c, value, 0.0).astype(BF16)


def _group_mean_matrix(width, group):
    return _group_matrix(width, group, 1.0 / group)


def _sigmoid(x):
    return 1.0 / (1.0 + jnp.exp(-x))


def _log_sigmoid(x):
    return jnp.minimum(x, 0.0) - jnp.log1p(jnp.exp(-jnp.abs(x)))


def _in_proj_kernel(x_ref, g_ref, w_ref, qn_ref, kn_ref, cos_ref, sin_ref, lbf_ref, lbb_ref,
                    qt_out, k_out, vat_out, hq_out, hv_out, lff_out, kf_out, lfb_out, kb_out, sg_out, u_out,
                    *, layer):
    x = x_ref[...]
    ms = jnp.mean(x * x, axis=-1, keepdims=True)
    h = (x * lax.rsqrt(ms + EPS) * g_ref[...]).astype(BF16)

    def proj(lo, width):
        return _dot(h, w_ref[:, lo:lo + width])

    gmat = _group_mean_matrix(LANES, HEAD_DIM)
    cos = cos_ref[...]
    sin = sin_ref[...]
    lane = lax.broadcasted_iota(jnp.int32, (1, LANES), 1)
    first_half = (lane % ROPE_AXIS_DIM) < (ROPE_AXIS_DIM // 2)

    def norm_rope(z, gain, scale):
        zn = z * lax.rsqrt(_split_dot(z * z, gmat, 2) + EPS) * gain
        half = ROPE_AXIS_DIM // 2
        partner = jnp.where(first_half, pltpu.roll(zn, LANES - half, 1), pltpu.roll(zn, half, 1))
        return (zn * cos + partner * sin) * scale

    zq = proj(_C_Q, ATT_WIDTH)
    q_scale = HEAD_DIM ** -0.5 * LOG2E
    for c in range(ATT_WIDTH // LANES):
        qc = norm_rope(zq[:, c * LANES:(c + 1) * LANES], qn_ref[...], q_scale)
        qt_out[c * LANES:(c + 1) * LANES, :] = jnp.transpose(qc).astype(BF16)

    k_out[...] = norm_rope(proj(_C_K, KV_WIDTH), kn_ref[...], 1.0).astype(BF16)

    zvt = jnp.transpose(proj(_C_V, KV_WIDTH)).astype(BF16)
    ones = jnp.ones((HEAD_DIM, zvt.shape[1]), BF16)
    for hh in range(ATT_KV_HEADS):
        vat_out[hh] = jnp.concatenate([zvt[hh * HEAD_DIM:(hh + 1) * HEAD_DIM, :], ones], axis=0)

    def lower_bound(lb_ref):
        p = lb_ref[...]
        p = jnp.exp(p - jnp.max(p, axis=0, keepdims=True))
        p = p / jnp.sum(p, axis=0, keepdims=True)
        lb = jnp.zeros((1, HGRN_WIDTH), F32)
        for i in range(1, layer + 1):
            lb = lb + p[i:i + 1, :]
        return lb

    def forget(z, lb):
        log_f = _log_sigmoid(z) + jnp.log1p(lb * jnp.exp(-z))
        return log_f, (1.0 - lb) * _sigmoid(-z)

    hq_out[...] = proj(_C_HQ, HGRN_WIDTH)
    hv_out[...] = proj(_C_HI, HGRN_WIDTH)
    lf, kk = forget(proj(_C_FF, HGRN_WIDTH), lower_bound(lbf_ref))
    lff_out[...] = lf
    kf_out[...] = kk
    lf, kk = forget(proj(_C_FB, HGRN_WIDTH), lower_bound(lbb_ref))
    lfb_out[...] = lf
    kb_out[...] = kk
    zg = proj(_C_HG, HGRN_WIDTH)
    sg_out[...] = zg * _sigmoid(zg)

    u_out[...] = proj(_C_CA, CONV_WIDTH) * _sigmoid(proj(_C_CG, CONV_WIDTH))


def _in_proj(x, g, w, qn, kn, cos, sin, lbf, lbb, *, layer, tm):
    B, L, D = x.shape
    nt = L // tm
    tok = lambda width: pl.BlockSpec((None, tm, width), lambda b, i: (b, i, 0))
    full = lambda a: pl.BlockSpec(a.shape, lambda b, i: (0,) * a.ndim)
    hg = jax.ShapeDtypeStruct((B, L, HGRN_WIDTH), F32)
    out_shape = (
        jax.ShapeDtypeStruct((B, ATT_WIDTH, L), BF16),
        jax.ShapeDtypeStruct((B, L, KV_WIDTH), BF16),
        jax.ShapeDtypeStruct((B, ATT_KV_HEADS, 2 * HEAD_DIM, L), BF16),
        hg, hg, hg, hg, hg, hg, hg,
        jax.ShapeDtypeStruct((B, L, CONV_WIDTH), F32),
    )
    out_specs = (
        pl.BlockSpec((None, ATT_WIDTH, tm), lambda b, i: (b, 0, i)),
        tok(KV_WIDTH),
        pl.BlockSpec((None, ATT_KV_HEADS, 2 * HEAD_DIM, tm), lambda b, i: (b, 0, 0, i)),
        tok(HGRN_WIDTH), tok(HGRN_WIDTH), tok(HGRN_WIDTH), tok(HGRN_WIDTH), tok(HGRN_WIDTH), tok(HGRN_WIDTH),
        tok(HGRN_WIDTH), tok(CONV_WIDTH),
    )
    in_specs = [
        tok(D), full(g), full(w), full(qn), full(kn),
        pl.BlockSpec((tm, LANES), lambda b, i: (i, 0)),
        pl.BlockSpec((tm, LANES), lambda b, i: (i, 0)),
        full(lbf), full(lbb),
    ]
    return pl.pallas_call(
        functools.partial(_in_proj_kernel, layer=layer),
        grid=(B, nt),
        in_specs=in_specs,
        out_specs=out_specs,
        out_shape=out_shape,
        compiler_params=pltpu.CompilerParams(
            dimension_semantics=("parallel", "parallel"), vmem_limit_bytes=VMEM_LIMIT),
        name=f"in_proj_{layer}",
    )(x, g, w, qn, kn, cos, sin, lbf, lbb)


def _attn_kernel(qt_ref, k_ref, vat_ref, o_ref, qs_ref, s_ref, smax_ref, m_ref, acc_ref, *, tq, tk, nk):
    h = pl.program_id(1)
    qt = qt_ref[...]
    q4 = jnp.concatenate(
        [qt[g * HEAD_DIM:(g + 1) * HEAD_DIM, :] for g in range(ATT_GROUP)], axis=1)
    row_head = lax.broadcasted_iota(jnp.int32, (KV_WIDTH, 1), 0) // HEAD_DIM
    qs_ref[...] = jnp.where(row_head == h, jnp.concatenate([q4] * ATT_KV_HEADS, axis=0), jnp.zeros((), BF16))
    m_ref[...] = jnp.full(m_ref.shape, -jnp.inf, F32)
    acc_ref[...] = jnp.zeros(acc_ref.shape, F32)

    def scores(j, slot):
        k0 = pl.multiple_of(j * tk, tk)
        s = _dot(k_ref[pl.ds(k0, tk), :], qs_ref[...])
        s_ref[slot] = s
        smax_ref[slot] = jnp.max(s, axis=0, keepdims=True)

    def softmax_pv(j, slot):
        k0 = pl.multiple_of(j * tk, tk)
        m_prev = m_ref[...]
        m_new = jnp.maximum(m_prev, smax_ref[slot])
        p = jnp.exp2(s_ref[slot] - m_new).astype(BF16)
        acc_ref[...] = jnp.exp2(m_prev - m_new) * acc_ref[...] + _dot(vat_ref[:, pl.ds(k0, tk)], p)
        m_ref[...] = m_new

    scores(0, 0)

    def body(jj, carry):
        j = 2 * jj
        scores(j + 1, 1)
        softmax_pv(j, 0)
        scores(j + 2, 0)
        softmax_pv(j + 1, 1)
        return carry

    lax.fori_loop(0, nk // 2 - 1, body, 0)
    scores(nk - 1, 1)
    softmax_pv(nk - 2, 0)
    softmax_pv(nk - 1, 1)
    acc = acc_ref[...]
    res = acc[:HEAD_DIM, :] / acc[HEAD_DIM:, :]
    o_ref[...] = jnp.concatenate(
        [jnp.transpose(res[:, g * tq:(g + 1) * tq]) for g in range(ATT_GROUP)], axis=1).astype(o_ref.dtype)


def _attention(qt, k, vat, *, tq, tk):
    B, _, L = qt.shape
    gw = ATT_GROUP * HEAD_DIM
    mq = ATT_GROUP * tq
    assert (L // tk) % 2 == 0, "the kv loop handles two chunks per trip"
    return pl.pallas_call(
        functools.partial(_attn_kernel, tq=tq, tk=tk, nk=L // tk),
        grid=(B, ATT_KV_HEADS, L // tq),
        in_specs=[
            pl.BlockSpec((None, gw, tq), lambda b, h, i: (b, h, i)),
            pl.BlockSpec((None, L, KV_WIDTH), lambda b, h, i: (b, 0, 0)),
            pl.BlockSpec((None, None, 2 * HEAD_DIM, L), lambda b, h, i: (b, h, 0, 0)),
        ],
        out_specs=pl.BlockSpec((None, tq, gw), lambda b, h, i: (b, i, h)),
        out_shape=jax.ShapeDtypeStruct((B, L, ATT_WIDTH), BF16),
        scratch_shapes=[
            pltpu.VMEM((KV_WIDTH, mq), BF16),
            pltpu.VMEM((2, tk, mq), F32),
            pltpu.VMEM((2, 1, mq), F32),
            pltpu.VMEM((1, mq), F32),
            pltpu.VMEM((2 * HEAD_DIM, mq), F32),
        ],
        compiler_params=pltpu.CompilerParams(
            dimension_semantics=("parallel", "parallel", "parallel"), vmem_limit_bytes=VMEM_LIMIT),
        name="attention",
    )(qt, k, vat)


def _hgrn_kernel(q_ref, v_ref, lf_ref, k_ref, o_ref, s_ref, *, T, reverse):
    C = HGRN_CHUNK
    W = HGRN_WIDTH

    @pl.when(pl.program_id(1) == 0)
    def _():
        s_ref[...] = jnp.zeros(s_ref.shape, F32)

    q = q_ref[...]
    v = v_ref[...]
    k = k_ref[...]
    lf = lf_ref[...]

    r = lax.broadcasted_iota(jnp.int32, (T, T), 0)
    c = lax.broadcasted_iota(jnp.int32, (T, T), 1)
    same = (r // C) == (c // C)
    tri = same & ((c >= r) if reverse else (c <= r))
    m_tri = jnp.where(tri, 1.0, 0.0).astype(BF16)
    m_all = jnp.where(same, 1.0, 0.0).astype(BF16)

    def left_split_dot(m, x):
        acc = None
        rem = x
        for _ in range(3):
            t = rem.astype(BF16)
            acc = _dot(m, t) if acc is None else acc + _dot(m, t)
            rem = rem - t.astype(F32)
        return acc

    b = left_split_dot(m_tri, lf)
    bl = left_split_dot(m_all, lf)

    pos = lax.broadcasted_iota(jnp.int32, (T, 1), 0) % C
    head_sum = _group_matrix(W, W // HGRN_HEADS, 1.0)
    acc = jnp.zeros((T, W), F32)
    for d in range(C):
        if d == 0:
            ks, bs, vs = k, b, v
        else:
            sh = (T - d) if reverse else d
            ks, bs, vs = (pltpu.roll(a, sh, 0) for a in (k, b, v))
        valid = (pos <= C - 1 - d) if reverse else (pos >= d)
        w = jnp.where(valid, q * ks * jnp.exp(b - bs), 0.0)
        acc = acc + _dot(w.astype(BF16), head_sum) * vs

    qt = (q * jnp.exp(b)).astype(BF16)
    kt = (k * jnp.exp(bl - b)).astype(BF16)
    dec = jnp.exp(bl)
    vb = v.astype(BF16)
    dh = W // HGRN_HEADS
    lane = lax.broadcasted_iota(jnp.int32, (1, W), 1)
    hmask = [(lane // dh) == hh for hh in range(HGRN_HEADS)]
    zero = jnp.zeros((), BF16)

    s = s_ref[...]
    order = range(T // C - 1, -1, -1) if reverse else range(T // C)
    for n in order:
        r0 = n * C
        qn = qt[r0:r0 + C]
        kn = kt[r0:r0 + C]
        vn = vb[r0:r0 + C]
        qexp = jnp.concatenate([jnp.where(hmask[hh], qn, zero) for hh in range(HGRN_HEADS)], axis=0)
        o4 = lax.dot_general(qexp, s.astype(BF16), (((1,), (1,)), ((), ())), preferred_element_type=F32)
        on = jnp.concatenate([o4[hh * C:(hh + 1) * C, :] for hh in range(HGRN_HEADS)], axis=1)
        o_ref[r0:r0 + C, :] = acc[r0:r0 + C] + on
        kexp = jnp.concatenate([jnp.where(hmask[hh], kn, zero) for hh in range(HGRN_HEADS)], axis=0)
        vexp = jnp.concatenate([vn[:, hh * dh:(hh + 1) * dh] for hh in range(HGRN_HEADS)], axis=0)
        u = lax.dot_general(vexp, kexp, (((0,), (0,)), ((), ())), preferred_element_type=F32)
        s = s * dec[r0:r0 + 1, :] + u
    s_ref[...] = s


def _hgrn(q, v, lf, k, *, T, reverse):
    B, L, W = q.shape
    nt = L // T
    idx = (lambda b, i: (b, nt - 1 - i, 0)) if reverse else (lambda b, i: (b, i, 0))
    spec = pl.BlockSpec((None, T, W), idx)
    return pl.pallas_call(
        functools.partial(_hgrn_kernel, T=T, reverse=reverse),
        grid=(B, nt),
        in_specs=[spec, spec, spec, spec],
        out_specs=spec,
        out_shape=jax.ShapeDtypeStruct((B, L, W), F32),
        scratch_shapes=[pltpu.VMEM((W // HGRN_HEADS, W), F32)],
        compiler_params=pltpu.CompilerParams(
            dimension_semantics=("parallel", "arbitrary"), vmem_limit_bytes=VMEM_LIMIT),
        name="hgrn_bwd" if reverse else "hgrn_fwd",
    )(q, v, lf, k)


def _out_mlp_kernel(x_ref, oa_ref, ofw_ref, obw_ref, sg_ref, hn_ref, up_ref, uc_ref, un_ref,
                    cw_ref, cb_ref, lg_ref, lb_ref, wo_ref, nm_ref, w1_ref, w2_ref, out_ref, ext_ref,
                    *, tm, ff_chunk):
    i = pl.program_id(1)
    nt = pl.num_programs(1)

    o = ofw_ref[...] + obw_ref[...]
    gmat = _group_mean_matrix(HGRN_WIDTH, HGRN_WIDTH // HGRN_HEADS)
    o_hg = o * lax.rsqrt(_split_dot(o * o, gmat, 2) + EPS) * hn_ref[...] * sg_ref[...]

    ext_ref[0:CONV_HALO, :] = jnp.where(i > 0, up_ref[...], 0.0)
    ext_ref[CONV_HALO:CONV_HALO + tm, :] = uc_ref[...]
    ext_ref[CONV_HALO + tm:, :] = jnp.where(i < nt - 1, un_ref[...], 0.0)
    y = jnp.zeros((tm, CONV_WIDTH), F32)
    for w in range(CONV_KERNEL):
        y = y + ext_ref[pl.ds(CONV_HALO - CONV_PAD + w, tm), :] * cw_ref[w:w + 1, :]
    y = y + cb_ref[...]
    mu = jnp.mean(y, axis=-1, keepdims=True)
    yc = y - mu
    yn = yc * lax.rsqrt(jnp.mean(yc * yc, axis=-1, keepdims=True) + EPS) * lg_ref[...] + lb_ref[...]
    o_cv = yn * _sigmoid(yn)

    mix = (_dot(oa_ref[...], wo_ref[0:ATT_WIDTH, :])
           + _dot(o_hg.astype(BF16), wo_ref[ATT_WIDTH:ATT_WIDTH + HGRN_WIDTH, :])
           + _dot(o_cv.astype(BF16), wo_ref[ATT_WIDTH + HGRN_WIDTH:, :]))
    x1 = x_ref[...] + mix

    ms = jnp.mean(x1 * x1, axis=-1, keepdims=True)
    hn = (x1 * lax.rsqrt(ms + EPS) * nm_ref[...]).astype(BF16)
    mlp = None
    d_ff = w1_ref.shape[1]
    for c0 in range(0, d_ff, ff_chunk):
        a = jnp.maximum(_dot(hn, w1_ref[:, c0:c0 + ff_chunk]), 0.0)
        part = _dot((a * a).astype(BF16), w2_ref[c0:c0 + ff_chunk, :])
        mlp = part if mlp is None else mlp + part
    out_ref[...] = x1 + mlp


def _out_mlp(x, oa, ofw, obw, sg, hn_g, u, cw, cb, lg, lb, wo, nm, w1, w2, *, tm):
    B, L, D = x.shape
    nt = L // tm
    hb = tm // CONV_HALO
    nhb = L // CONV_HALO
    tok = lambda width: pl.BlockSpec((None, tm, width), lambda b, i: (b, i, 0))
    const = lambda a: pl.BlockSpec(a.shape, lambda b, i: (0,) * a.ndim)
    in_specs = [
        tok(D), tok(ATT_WIDTH), tok(HGRN_WIDTH), tok(HGRN_WIDTH), tok(HGRN_WIDTH), const(hn_g),
        pl.BlockSpec((None, CONV_HALO, CONV_WIDTH), lambda b, i: (b, jnp.maximum(i * hb - 1, 0), 0)),
        tok(CONV_WIDTH),
        pl.BlockSpec((None, CONV_HALO, CONV_WIDTH), lambda b, i: (b, jnp.minimum((i + 1) * hb, nhb - 1), 0)),
        const(cw), const(cb), const(lg), const(lb), const(wo), const(nm), const(w1), const(w2),
    ]
    return pl.pallas_call(
        functools.partial(_out_mlp_kernel, tm=tm, ff_chunk=min(1024, w1.shape[1])),
        grid=(B, nt),
        in_specs=in_specs,
        out_specs=tok(D),
        out_shape=jax.ShapeDtypeStruct((B, L, D), F32),
        scratch_shapes=[pltpu.VMEM((tm + 2 * CONV_HALO, CONV_WIDTH), F32)],
        compiler_params=pltpu.CompilerParams(
            dimension_semantics=("parallel", "parallel"), vmem_limit_bytes=VMEM_LIMIT),
        name="out_mlp",
    )(x, oa, ofw, obw, sg, hn_g, u, u, u, cw, cb, lg, lb, wo, nm, w1, w2)


def _rope_tables(L):
    rows = L // GRID_W
    row_idx = jnp.repeat(jnp.arange(rows), GRID_W)
    col_idx = jnp.tile(jnp.arange(GRID_W), rows)
    inv_freq = ROPE_THETA ** (-jnp.arange(0, ROPE_AXIS_DIM, 2, dtype=F32) / ROPE_AXIS_DIM)
    ar = row_idx.astype(F32)[:, None] * inv_freq[None, :]
    ac = col_idx.astype(F32)[:, None] * inv_freq[None, :]
    cos = jnp.concatenate([jnp.cos(ar), jnp.cos(ar), jnp.cos(ac), jnp.cos(ac)], axis=-1)
    sin = jnp.concatenate([-jnp.sin(ar), jnp.sin(ar), -jnp.sin(ac), jnp.sin(ac)], axis=-1)
    reps = LANES // HEAD_DIM
    return jnp.tile(cos, (1, reps)), jnp.tile(sin, (1, reps))


def _tile(L, want):
    t = min(want, L)
    assert L % t == 0, (L, t)
    return t


def kernel(x, w_in, w_out, norm_mix, norm_mlp, q_norm, k_norm, hgrn_lb_fwd, hgrn_lb_bwd, hgrn_norm,
           conv_w, conv_b, conv_ln_g, conv_ln_b, w_mlp_in, w_mlp_out):
    B, L, D = x.shape
    depth = w_in.shape[0]
    assert w_in.shape[2] == IN_COLS and L % GRID_W == 0
    cos, sin = _rope_tables(L)
    row = lambda a: a.reshape(1, -1).astype(F32)
    reps = LANES // HEAD_DIM
    tm_in = _tile(L, 512)
    tq = _tile(L, 512)
    tk = _tile(L // 2, 512)
    t_hg = _tile(L, 256)
    tm_out = _tile(L, 512)

    for l in range(depth):
        (q, kt, va, hq, hv, lff, kf, lfb, kb, sg, u) = _in_proj(
            x, row(norm_mix[l]), w_in[l].astype(BF16),
            jnp.tile(row(q_norm[l]), (1, reps)), jnp.tile(row(k_norm[l]), (1, reps)),
            cos, sin, hgrn_lb_fwd.astype(F32), hgrn_lb_bwd.astype(F32), layer=l, tm=tm_in)
        o_att = _attention(q, kt, va, tq=tq, tk=tk)
        o_fw = _hgrn(hq, hv, lff, kf, T=t_hg, reverse=False)
        o_bw = _hgrn(hq, hv, lfb, kb, T=t_hg, reverse=True)
        x = _out_mlp(
            x, o_att, o_fw, o_bw, sg, row(hgrn_norm[l]), u,
            conv_w[l].astype(F32), row(conv_b[l]), row(conv_ln_g[l]), row(conv_ln_b[l]),
            w_out[l].astype(BF16), row(norm_mlp[l]), w_mlp_in[l].astype(BF16), w_mlp_out[l].astype(BF16),
            tm=tm_out)
    return x
```

```python
import functools
import math

import jax
import jax.numpy as jnp
from jax import lax
from jax.experimental import pallas as pl
from jax.experimental.pallas import tpu as pltpu

F32 = jnp.float32
BF16 = jnp.bfloat16

EPS = 1e-6
GRID_W = 64
HEAD_DIM = 64
ATT_HEADS = 8
ATT_KV_HEADS = 2
ATT_GROUP = ATT_HEADS // ATT_KV_HEADS
ATT_WIDTH = ATT_HEADS * HEAD_DIM
KV_WIDTH = ATT_KV_HEADS * HEAD_DIM
ROPE_THETA = 10000.0
ROPE_AXIS_DIM = HEAD_DIM // 2
HGRN_WIDTH = 256
HGRN_HEADS = 4
HGRN_CHUNK = 16
CONV_WIDTH = 256
CONV_KERNEL = 31
CONV_PAD = (CONV_KERNEL - 1) // 2
CONV_HALO = 16
KV_UNROLL = 4

LANES = 128
LOG2E = 1.4426950408889634
VMEM_LIMIT = 56 * 1024 * 1024

_C_Q = 0
_C_K = _C_Q + ATT_WIDTH
_C_V = _C_K + KV_WIDTH
_C_HQ = _C_V + KV_WIDTH
_C_HI = _C_HQ + HGRN_WIDTH
_C_FF = _C_HI + HGRN_WIDTH
_C_FB = _C_FF + HGRN_WIDTH
_C_HG = _C_FB + HGRN_WIDTH
_C_CA = _C_HG + HGRN_WIDTH
_C_CG = _C_CA + CONV_WIDTH
IN_COLS = _C_CG + CONV_WIDTH


def _dot(a, b):
    return jnp.dot(a, b, preferred_element_type=F32)


def _split_dot(x, m, parts):
    acc = None
    r = x
    for _ in range(parts):
        t = r.astype(BF16)
        acc = _dot(t, m) if acc is None else acc + _dot(t, m)
        r = r - t.astype(F32)
    return acc


def _group_matrix(width, group, value):
    r = lax.broadcasted_iota(jnp.int32, (width, width), 0) // group
    c = lax.broadcasted_iota(jnp.int32, (width, width), 1) // group
    return jnp.where(r == c, value, 0.0).astype(BF16)


def _group_mean_matrix(width, group):
    return _group_matrix(width, group, 1.0 / group)


def _sigmoid(x):
    return 1.0 / (1.0 + jnp.exp(-x))


def _log_sigmoid(x):
    return jnp.minimum(x, 0.0) - jnp.log1p(jnp.exp(-jnp.abs(x)))


def _in_proj_kernel(x_ref, g_ref, w_ref, qn_ref, kn_ref, cos_ref, sin_ref, lbf_ref, lbb_ref,
                    qt_out, k_out, vat_out, hq_out, hv_out, lff_out, kf_out, lfb_out, kb_out, sg_out, u_out,
                    *, layer):
    x = x_ref[...]
    ms = jnp.mean(x * x, axis=-1, keepdims=True)
    h = (x * lax.rsqrt(ms + EPS) * g_ref[...]).astype(BF16)

    def proj(lo, width):
        return _dot(h, w_ref[:, lo:lo + width])

    gmat = _group_mean_matrix(LANES, HEAD_DIM)
    cos = cos_ref[...]
    sin = sin_ref[...]
    lane = lax.broadcasted_iota(jnp.int32, (1, LANES), 1)
    first_half = (lane % ROPE_AXIS_DIM) < (ROPE_AXIS_DIM // 2)

    def norm_rope(z, gain, scale):
        zn = z * lax.rsqrt(_split_dot(z * z, gmat, 2) + EPS) * gain
        half = ROPE_AXIS_DIM // 2
        partner = jnp.where(first_half, pltpu.roll(zn, LANES - half, 1), pltpu.roll(zn, half, 1))
        return (zn * cos + partner * sin) * scale

    zq = proj(_C_Q, ATT_WIDTH)
    q_scale = HEAD_DIM ** -0.5 * LOG2E
    for c in range(ATT_WIDTH // LANES):
        qc = norm_rope(zq[:, c * LANES:(c + 1) * LANES], qn_ref[...], q_scale)
        qt_out[c * LANES:(c + 1) * LANES, :] = jnp.transpose(qc).astype(BF16)

    k_out[...] = norm_rope(proj(_C_K, KV_WIDTH), kn_ref[...], 1.0).astype(BF16)

    zvt = jnp.transpose(proj(_C_V, KV_WIDTH)).astype(BF16)
    ones = jnp.ones((HEAD_DIM, zvt.shape[1]), BF16)
    for hh in range(ATT_KV_HEADS):
        vat_out[hh] = jnp.concatenate([zvt[hh * HEAD_DIM:(hh + 1) * HEAD_DIM, :], ones], axis=0)

    def lower_bound(lb_ref):
        p = lb_ref[...]
        p = jnp.exp(p - jnp.max(p, axis=0, keepdims=True))
        p = p / jnp.sum(p, axis=0, keepdims=True)
        lb = jnp.zeros((1, HGRN_WIDTH), F32)
        for i in range(1, layer + 1):
            lb = lb + p[i:i + 1, :]
        return lb

    def forget(z, lb):
        log_f = _log_sigmoid(z) + jnp.log1p(lb * jnp.exp(-z))
        return log_f, (1.0 - lb) * _sigmoid(-z)

    hq_out[...] = proj(_C_HQ, HGRN_WIDTH)
    hv_out[...] = proj(_C_HI, HGRN_WIDTH)
    lf, kk = forget(proj(_C_FF, HGRN_WIDTH), lower_bound(lbf_ref))
    lff_out[...] = lf
    kf_out[...] = kk
    lf, kk = forget(proj(_C_FB, HGRN_WIDTH), lower_bound(lbb_ref))
    lfb_out[...] = lf
    kb_out[...] = kk
    zg = proj(_C_HG, HGRN_WIDTH)
    sg_out[...] = zg * _sigmoid(zg)

    u_out[...] = proj(_C_CA, CONV_WIDTH) * _sigmoid(proj(_C_CG, CONV_WIDTH))


def _in_proj(x, g, w, qn, kn, cos, sin, lbf, lbb, *, layer, tm):
    B, L, D = x.shape
    nt = L // tm
    tok = lambda width: pl.BlockSpec((None, tm, width), lambda b, i: (b, i, 0))
    full = lambda a: pl.BlockSpec(a.shape, lambda b, i: (0,) * a.ndim)
    hg = jax.ShapeDtypeStruct((B, L, HGRN_WIDTH), F32)
    out_shape = (
        jax.ShapeDtypeStruct((B, ATT_WIDTH, L), BF16),
        jax.ShapeDtypeStruct((B, L, KV_WIDTH), BF16),
        jax.ShapeDtypeStruct((B, ATT_KV_HEADS, 2 * HEAD_DIM, L), BF16),
        hg, hg, hg, hg, hg, hg, hg,
        jax.ShapeDtypeStruct((B, L, CONV_WIDTH), F32),
    )
    out_specs = (
        pl.BlockSpec((None, ATT_WIDTH, tm), lambda b, i: (b, 0, i)),
        tok(KV_WIDTH),
        pl.BlockSpec((None, ATT_KV_HEADS, 2 * HEAD_DIM, tm), lambda b, i: (b, 0, 0, i)),
        tok(HGRN_WIDTH), tok(HGRN_WIDTH), tok(HGRN_WIDTH), tok(HGRN_WIDTH), tok(HGRN_WIDTH), tok(HGRN_WIDTH),
        tok(HGRN_WIDTH), tok(CONV_WIDTH),
    )
    in_specs = [
        tok(D), full(g), full(w), full(qn), full(kn),
        pl.BlockSpec((tm, LANES), lambda b, i: (i, 0)),
        pl.BlockSpec((tm, LANES), lambda b, i: (i, 0)),
        full(lbf), full(lbb),
    ]
    return pl.pallas_call(
        functools.partial(_in_proj_kernel, layer=layer),
        grid=(B, nt),
        in_specs=in_specs,
        out_specs=out_specs,
        out_shape=out_shape,
        compiler_params=pltpu.CompilerParams(
            dimension_semantics=("parallel", "parallel"), vmem_limit_bytes=VMEM_LIMIT),
        name=f"in_proj_{layer}",
    )(x, g, w, qn, kn, cos, sin, lbf, lbb)


def _attn_kernel(qt_ref, k_ref, vat_ref, o_ref, qs_ref, s_ref, smax_ref, m_ref, acc_ref, *, tq, tk, nk):
    h = pl.program_id(1)
    qt = qt_ref[...]
    q4 = jnp.concatenate(
        [qt[g * HEAD_DIM:(g + 1) * HEAD_DIM, :] for g in range(ATT_GROUP)], axis=1)
    row_head = lax.broadcasted_iota(jnp.int32, (KV_WIDTH, 1), 0) // HEAD_DIM
    qs_ref[...] = jnp.where(row_head == h, jnp.concatenate([q4] * ATT_KV_HEADS, axis=0), jnp.zeros((), BF16))
    m_ref[...] = jnp.full(m_ref.shape, -jnp.inf, F32)
    acc_ref[...] = jnp.zeros(acc_ref.shape, F32)

    def scores(j, g, slot):
        k0 = pl.multiple_of(j * tk, tk)
        cols = slice(g * tq, (g + 1) * tq)
        s = _dot(k_ref[pl.ds(k0, tk), :], qs_ref[:, cols])
        s_ref[:, cols] = s
        smax_ref[slot, :, cols] = jnp.max(s, axis=0, keepdims=True)

    def step(j, slot, last):
        k0 = pl.multiple_of(j * tk, tk)
        m_prev = m_ref[...]
        m_new = jnp.maximum(m_prev, smax_ref[slot])
        alpha = jnp.exp2(m_prev - m_new)
        vat = vat_ref[:, pl.ds(k0, tk)]
        for g in range(ATT_GROUP):
            cols = slice(g * tq, (g + 1) * tq)
            p = jnp.exp2(s_ref[:, cols] - m_new[:, cols]).astype(BF16)
            acc_ref[:, cols] = alpha[:, cols] * acc_ref[:, cols] + _dot(vat, p)
            if not last:
                scores(j + 1, g, 1 - slot)
        m_ref[...] = m_new

    for g in range(ATT_GROUP):
        scores(0, g, 0)

    def body(jj, carry):
        for u in range(KV_UNROLL):
            step(KV_UNROLL * jj + u, u % 2, False)
        return carry

    lax.fori_loop(0, nk // KV_UNROLL - 1, body, 0)
    for u in range(KV_UNROLL):
        step(nk - KV_UNROLL + u, u % 2, u == KV_UNROLL - 1)
    acc = acc_ref[...]
    res = acc[:HEAD_DIM, :] / acc[HEAD_DIM:, :]
    o_ref[...] = jnp.concatenate(
        [jnp.transpose(res[:, g * tq:(g + 1) * tq]) for g in range(ATT_GROUP)], axis=1).astype(o_ref.dtype)


def _attention(qt, k, vat, *, tq, tk):
    B, _, L = qt.shape
    gw = ATT_GROUP * HEAD_DIM
    mq = ATT_GROUP * tq
    assert (L // tk) % KV_UNROLL == 0, "the kv loop handles KV_UNROLL chunks per trip"
    return pl.pallas_call(
        functools.partial(_attn_kernel, tq=tq, tk=tk, nk=L // tk),
        grid=(B, ATT_KV_HEADS, L // tq),
        in_specs=[
            pl.BlockSpec((None, gw, tq), lambda b, h, i: (b, h, i)),
            pl.BlockSpec((None, L, KV_WIDTH), lambda b, h, i: (b, 0, 0)),
            pl.BlockSpec((None, None, 2 * HEAD_DIM, L), lambda b, h, i: (b, h, 0, 0)),
        ],
        out_specs=pl.BlockSpec((None, tq, gw), lambda b, h, i: (b, i, h)),
        out_shape=jax.ShapeDtypeStruct((B, L, ATT_WIDTH), BF16),
        scratch_shapes=[
            pltpu.VMEM((KV_WIDTH, mq), BF16),
            pltpu.VMEM((tk, mq), F32),
            pltpu.VMEM((2, 1, mq), F32),
            pltpu.VMEM((1, mq), F32),
            pltpu.VMEM((2 * HEAD_DIM, mq), F32),
        ],
        compiler_params=pltpu.CompilerParams(
            dimension_semantics=("parallel", "parallel", "parallel"), vmem_limit_bytes=VMEM_LIMIT),
        name="attention",
    )(qt, k, vat)


def _hgrn_kernel(q_ref, v_ref, lf_ref, k_ref, o_ref, s_ref, *, T, reverse):
    C = HGRN_CHUNK
    W = HGRN_WIDTH

    @pl.when(pl.program_id(1) == 0)
    def _():
        s_ref[...] = jnp.zeros(s_ref.shape, F32)

    q = q_ref[...]
    v = v_ref[...]
    k = k_ref[...]
    lf = lf_ref[...]

    r = lax.broadcasted_iota(jnp.int32, (T, T), 0)
    c = lax.broadcasted_iota(jnp.int32, (T, T), 1)
    same = (r // C) == (c // C)
    tri = same & ((c >= r) if reverse else (c <= r))
    m_tri = jnp.where(tri, 1.0, 0.0).astype(BF16)
    m_all = jnp.where(same, 1.0, 0.0).astype(BF16)

    def left_split_dot(m, x):
        acc = None
        rem = x
        for _ in range(3):
            t = rem.astype(BF16)
            acc = _dot(m, t) if acc is None else acc + _dot(m, t)
            rem = rem - t.astype(F32)
        return acc

    b = left_split_dot(m_tri, lf)
    bl = left_split_dot(m_all, lf)

    pos = lax.broadcasted_iota(jnp.int32, (T, 1), 0) % C
    head_sum = _group_matrix(W, W // HGRN_HEADS, 1.0)
    acc = jnp.zeros((T, W), F32)
    for d in range(C):
        if d == 0:
            ks, bs, vs = k, b, v
        else:
            sh = (T - d) if reverse else d
            ks, bs, vs = (pltpu.roll(a, sh, 0) for a in (k, b, v))
        valid = (pos <= C - 1 - d) if reverse else (pos >= d)
        w = jnp.where(valid, q * ks * jnp.exp(b - bs), 0.0)
        acc = acc + _dot(w.astype(BF16), head_sum) * vs

    qt = (q * jnp.exp(b)).astype(BF16)
    kt = (k * jnp.exp(bl - b)).astype(BF16)
    dec = jnp.exp(bl)
    vb = v.astype(BF16)
    dh = W // HGRN_HEADS
    lane = lax.broadcasted_iota(jnp.int32, (1, W), 1)
    hmask = [(lane // dh) == hh for hh in range(HGRN_HEADS)]
    zero = jnp.zeros((), BF16)

    s = s_ref[...]
    order = range(T // C - 1, -1, -1) if reverse else range(T // C)
    for n in order:
        r0 = n * C
        qn = qt[r0:r0 + C]
        kn = kt[r0:r0 + C]
        vn = vb[r0:r0 + C]
        qexp = jnp.concatenate([jnp.where(hmask[hh], qn, zero) for hh in range(HGRN_HEADS)], axis=0)
        o4 = lax.dot_general(qexp, s.astype(BF16), (((1,), (1,)), ((), ())), preferred_element_type=F32)
        on = jnp.concatenate([o4[hh * C:(hh + 1) * C, :] for hh in range(HGRN_HEADS)], axis=1)
        o_ref[r0:r0 + C, :] = acc[r0:r0 + C] + on
        kexp = jnp.concatenate([jnp.where(hmask[hh], kn, zero) for hh in range(HGRN_HEADS)], axis=0)
        vexp = jnp.concatenate([vn[:, hh * dh:(hh + 1) * dh] for hh in range(HGRN_HEADS)], axis=0)
        u = lax.dot_general(vexp, kexp, (((0,), (0,)), ((), ())), preferred_element_type=F32)
        s = s * dec[r0:r0 + 1, :] + u
    s_ref[...] = s


def _hgrn(q, v, lf, k, *, T, reverse):
    B, L, W = q.shape
    nt = L // T
    idx = (lambda b, i: (b, nt - 1 - i, 0)) if reverse else (lambda b, i: (b, i, 0))
    spec = pl.BlockSpec((None, T, W), idx)
    return pl.pallas_call(
        functools.partial(_hgrn_kernel, T=T, reverse=reverse),
        grid=(B, nt),
        in_specs=[spec, spec, spec, spec],
        out_specs=spec,
        out_shape=jax.ShapeDtypeStruct((B, L, W), F32),
        scratch_shapes=[pltpu.VMEM((W // HGRN_HEADS, W), F32)],
        compiler_params=pltpu.CompilerParams(
            dimension_semantics=("parallel", "arbitrary"), vmem_limit_bytes=VMEM_LIMIT),
        name="hgrn_bwd" if reverse else "hgrn_fwd",
    )(q, v, lf, k)


def _out_mlp_kernel(x_ref, oa_ref, ofw_ref, obw_ref, sg_ref, hn_ref, up_ref, uc_ref, un_ref,
                    cw_ref, cb_ref, lg_ref, lb_ref, wo_ref, nm_ref, w1_ref, w2_ref, out_ref, ext_ref,
                    *, tm, ff_chunk):
    i = pl.program_id(1)
    nt = pl.num_programs(1)

    o = ofw_ref[...] + obw_ref[...]
    gmat = _group_mean_matrix(HGRN_WIDTH, HGRN_WIDTH // HGRN_HEADS)
    o_hg = o * lax.rsqrt(_split_dot(o * o, gmat, 2) + EPS) * hn_ref[...] * sg_ref[...]

    ext_ref[0:CONV_HALO, :] = jnp.where(i > 0, up_ref[...], 0.0)
    ext_ref[CONV_HALO:CONV_HALO + tm, :] = uc_ref[...]
    ext_ref[CONV_HALO + tm:, :] = jnp.where(i < nt - 1, un_ref[...], 0.0)
    y = jnp.zeros((tm, CONV_WIDTH), F32)
    for w in range(CONV_KERNEL):
        y = y + ext_ref[pl.ds(CONV_HALO - CONV_PAD + w, tm), :] * cw_ref[w:w + 1, :]
    y = y + cb_ref[...]
    mu = jnp.mean(y, axis=-1, keepdims=True)
    yc = y - mu
    yn = yc * lax.rsqrt(jnp.mean(yc * yc, axis=-1, keepdims=True) + EPS) * lg_ref[...] + lb_ref[...]
    o_cv = yn * _sigmoid(yn)

    mix = (_dot(oa_ref[...], wo_ref[0:ATT_WIDTH, :])
           + _dot(o_hg.astype(BF16), wo_ref[ATT_WIDTH:ATT_WIDTH + HGRN_WIDTH, :])
           + _dot(o_cv.astype(BF16), wo_ref[ATT_WIDTH + HGRN_WIDTH:, :]))
    x1 = x_ref[...] + mix

    ms = jnp.mean(x1 * x1, axis=-1, keepdims=True)
    hn = (x1 * lax.rsqrt(ms + EPS) * nm_ref[...]).astype(BF16)
    mlp = None
    d_ff = w1_ref.shape[1]
    for c0 in range(0, d_ff, ff_chunk):
        a = jnp.maximum(_dot(hn, w1_ref[:, c0:c0 + ff_chunk]), 0.0)
        part = _dot((a * a).astype(BF16), w2_ref[c0:c0 + ff_chunk, :])
        mlp = part if mlp is None else mlp + part
    out_ref[...] = x1 + mlp


def _out_mlp(x, oa, ofw, obw, sg, hn_g, u, cw, cb, lg, lb, wo, nm, w1, w2, *, tm):
    B, L, D = x.shape
    nt = L // tm
    hb = tm // CONV_HALO
    nhb = L // CONV_HALO
    tok = lambda width: pl.BlockSpec((None, tm, width), lambda b, i: (b, i, 0))
    const = lambda a: pl.BlockSpec(a.shape, lambda b, i: (0,) * a.ndim)
    in_specs = [
        tok(D), tok(ATT_WIDTH), tok(HGRN_WIDTH), tok(HGRN_WIDTH), tok(HGRN_WIDTH), const(hn_g),
        pl.BlockSpec((None, CONV_HALO, CONV_WIDTH), lambda b, i: (b, jnp.maximum(i * hb - 1, 0), 0)),
        tok(CONV_WIDTH),
        pl.BlockSpec((None, CONV_HALO, CONV_WIDTH), lambda b, i: (b, jnp.minimum((i + 1) * hb, nhb - 1), 0)),
        const(cw), const(cb), const(lg), const(lb), const(wo), const(nm), const(w1), const(w2),
    ]
    return pl.pallas_call(
        functools.partial(_out_mlp_kernel, tm=tm, ff_chunk=min(1024, w1.shape[1])),
        grid=(B, nt),
        in_specs=in_specs,
        out_specs=tok(D),
        out_shape=jax.ShapeDtypeStruct((B, L, D), F32),
        scratch_shapes=[pltpu.VMEM((tm + 2 * CONV_HALO, CONV_WIDTH), F32)],
        compiler_params=pltpu.CompilerParams(
            dimension_semantics=("parallel", "parallel"), vmem_limit_bytes=VMEM_LIMIT),
        name="out_mlp",
    )(x, oa, ofw, obw, sg, hn_g, u, u, u, cw, cb, lg, lb, wo, nm, w1, w2)


def _rope_tables(L):
    rows = L // GRID_W
    row_idx = jnp.repeat(jnp.arange(rows), GRID_W)
    col_idx = jnp.tile(jnp.arange(GRID_W), rows)
    inv_freq = ROPE_THETA ** (-jnp.arange(0, ROPE_AXIS_DIM, 2, dtype=F32) / ROPE_AXIS_DIM)
    ar = row_idx.astype(F32)[:, None] * inv_freq[None, :]
    ac = col_idx.astype(F32)[:, None] * inv_freq[None, :]
    cos = jnp.concatenate([jnp.cos(ar), jnp.cos(ar), jnp.cos(ac), jnp.cos(ac)], axis=-1)
    sin = jnp.concatenate([-jnp.sin(ar), jnp.sin(ar), -jnp.sin(ac), jnp.sin(ac)], axis=-1)
    reps = LANES // HEAD_DIM
    return jnp.tile(cos, (1, reps)), jnp.tile(sin, (1, reps))


def _tile(L, want):
    t = min(want, L)
    assert L % t == 0, (L, t)
    return t


def kernel(x, w_in, w_out, norm_mix, norm_mlp, q_norm, k_norm, hgrn_lb_fwd, hgrn_lb_bwd, hgrn_norm,
           conv_w, conv_b, conv_ln_g, conv_ln_b, w_mlp_in, w_mlp_out):
    B, L, D = x.shape
    depth = w_in.shape[0]
    assert w_in.shape[2] == IN_COLS and L % GRID_W == 0
    cos, sin = _rope_tables(L)
    row = lambda a: a.reshape(1, -1).astype(F32)
    reps = LANES // HEAD_DIM
    tm_in = _tile(L, 512)
    tq = _tile(L, 512)
    tk = _tile(L // KV_UNROLL, 512)
    t_hg = _tile(L, 256)
    tm_out = _tile(L, 512)

    for l in range(depth):
        (q, kt, va, hq, hv, lff, kf, lfb, kb, sg, u) = _in_proj(
            x, row(norm_mix[l]), w_in[l].astype(BF16),
            jnp.tile(row(q_norm[l]), (1, reps)), jnp.tile(row(k_norm[l]), (1, reps)),
            cos, sin, hgrn_lb_fwd.astype(F32), hgrn_lb_bwd.astype(F32), layer=l, tm=tm_in)
        o_att = _attention(q, kt, va, tq=tq, tk=tk)
        o_fw = _hgrn(hq, hv, lff, kf, T=t_hg, reverse=False)
        o_bw = _hgrn(hq, hv, lfb, kb, T=t_hg, reverse=True)
        x = _out_mlp(
            x, o_att, o_fw, o_bw, sg, row(hgrn_norm[l]), u,
            conv_w[l].astype(F32), row(conv_b[l]), row(conv_ln_g[l]), row(conv_ln_b[l]),
            w_out[l].astype(BF16), row(norm_mlp[l]), w_mlp_in[l].astype(BF16), w_mlp_out[l].astype(BF16),
            tm=tm_out)
    return x
```

```python
import functools
import math

import jax
import jax.numpy as jnp
from jax import lax
from jax.experimental import pallas as pl
from jax.experimental.pallas import tpu as pltpu

F32 = jnp.float32
BF16 = jnp.bfloat16

EPS = 1e-6
GRID_W = 64
HEAD_DIM = 64
ATT_HEADS = 8
ATT_KV_HEADS = 2
ATT_GROUP = ATT_HEADS // ATT_KV_HEADS
ATT_WIDTH = ATT_HEADS * HEAD_DIM
KV_WIDTH = ATT_KV_HEADS * HEAD_DIM
ROPE_THETA = 10000.0
ROPE_AXIS_DIM = HEAD_DIM // 2
HGRN_WIDTH = 256
HGRN_HEADS = 4
HGRN_CHUNK = 16
CONV_WIDTH = 256
CONV_KERNEL = 31
CONV_PAD = (CONV_KERNEL - 1) // 2
CONV_HALO = 16
KV_UNROLL = 8

LANES = 128
SUBLANES = 8
LOG2E = 1.4426950408889634
VMEM_LIMIT = 56 * 1024 * 1024

_C_Q = 0
_C_K = _C_Q + ATT_WIDTH
_C_V = _C_K + KV_WIDTH
_C_HQ = _C_V + KV_WIDTH
_C_HI = _C_HQ + HGRN_WIDTH
_C_FF = _C_HI + HGRN_WIDTH
_C_FB = _C_FF + HGRN_WIDTH
_C_HG = _C_FB + HGRN_WIDTH
_C_CA = _C_HG + HGRN_WIDTH
_C_CG = _C_CA + CONV_WIDTH
IN_COLS = _C_CG + CONV_WIDTH


def _dot(a, b):
    return jnp.dot(a, b, preferred_element_type=F32)


def _split_dot(x, m, parts):
    acc = None
    r = x
    for _ in range(parts):
        t = r.astype(BF16)
        acc = _dot(t, m) if acc is None else acc + _dot(t, m)
        r = r - t.astype(F32)
    return acc


def _group_matrix(width, group, value):
    r = lax.broadcasted_iota(jnp.int32, (width, width), 0) // group
    c = lax.broadcasted_iota(jnp.int32, (width, width), 1) // group
    return jnp.where(r == c, value, 0.0).astype(BF16)


def _group_mean_matrix(width, group):
    return _group_matrix(width, group, 1.0 / group)


def _sigmoid(x):
    return 1.0 / (1.0 + jnp.exp(-x))


def _log_sigmoid(x):
    return jnp.minimum(x, 0.0) - jnp.log1p(jnp.exp(-jnp.abs(x)))


def _in_proj_kernel(x_ref, g_ref, w_ref, qn_ref, kn_ref, cos_ref, sin_ref, lbf_ref, lbb_ref,
                    qt_out, k_out, vat_out, hq_out, hv_out, lff_out, kf_out, lfb_out, kb_out, sg_out, u_out,
                    *, layer):
    x = x_ref[...]
    ms = jnp.mean(x * x, axis=-1, keepdims=True)
    h = (x * lax.rsqrt(ms + EPS) * g_ref[...]).astype(BF16)

    def proj(lo, width):
        return _dot(h, w_ref[:, lo:lo + width])

    gmat = _group_mean_matrix(LANES, HEAD_DIM)
    cos = cos_ref[...]
    sin = sin_ref[...]
    lane = lax.broadcasted_iota(jnp.int32, (1, LANES), 1)
    first_half = (lane % ROPE_AXIS_DIM) < (ROPE_AXIS_DIM // 2)

    def norm_rope(z, gain, scale):
        zn = z * lax.rsqrt(_split_dot(z * z, gmat, 2) + EPS) * gain
        half = ROPE_AXIS_DIM // 2
        partner = jnp.where(first_half, pltpu.roll(zn, LANES - half, 1), pltpu.roll(zn, half, 1))
        return (zn * cos + partner * sin) * scale

    zq = proj(_C_Q, ATT_WIDTH)
    q_scale = HEAD_DIM ** -0.5 * LOG2E
    for c in range(ATT_WIDTH // LANES):
        qc = norm_rope(zq[:, c * LANES:(c + 1) * LANES], qn_ref[...], q_scale)
        qt_out[c * LANES:(c + 1) * LANES, :] = jnp.transpose(qc).astype(BF16)

    k_out[...] = norm_rope(proj(_C_K, KV_WIDTH), kn_ref[...], 1.0).astype(BF16)

    zvt = jnp.transpose(proj(_C_V, KV_WIDTH)).astype(BF16)
    ones = jnp.ones((HEAD_DIM, zvt.shape[1]), BF16)
    for hh in range(ATT_KV_HEADS):
        vat_out[hh] = jnp.concatenate([zvt[hh * HEAD_DIM:(hh + 1) * HEAD_DIM, :], ones], axis=0)

    def lower_bound(lb_ref):
        p = lb_ref[...]
        p = jnp.exp(p - jnp.max(p, axis=0, keepdims=True))
        p = p / jnp.sum(p, axis=0, keepdims=True)
        lb = jnp.zeros((1, HGRN_WIDTH), F32)
        for i in range(1, layer + 1):
            lb = lb + p[i:i + 1, :]
        return lb

    def forget(z, lb):
        log_f = _log_sigmoid(z) + jnp.log1p(lb * jnp.exp(-z))
        return log_f, (1.0 - lb) * _sigmoid(-z)

    hq_out[...] = proj(_C_HQ, HGRN_WIDTH)
    hv_out[...] = proj(_C_HI, HGRN_WIDTH)
    lf, kk = forget(proj(_C_FF, HGRN_WIDTH), lower_bound(lbf_ref))
    lff_out[...] = lf
    kf_out[...] = kk
    lf, kk = forget(proj(_C_FB, HGRN_WIDTH), lower_bound(lbb_ref))
    lfb_out[...] = lf
    kb_out[...] = kk
    zg = proj(_C_HG, HGRN_WIDTH)
    sg_out[...] = zg * _sigmoid(zg)

    u_out[...] = proj(_C_CA, CONV_WIDTH) * _sigmoid(proj(_C_CG, CONV_WIDTH))


def _in_proj(x, g, w, qn, kn, cos, sin, lbf, lbb, *, layer, tm):
    B, L, D = x.shape
    nt = L // tm
    tok = lambda width: pl.BlockSpec((None, tm, width), lambda b, i: (b, i, 0))
    full = lambda a: pl.BlockSpec(a.shape, lambda b, i: (0,) * a.ndim)
    hg = jax.ShapeDtypeStruct((B, L, HGRN_WIDTH), F32)
    out_shape = (
        jax.ShapeDtypeStruct((B, ATT_WIDTH, L), BF16),
        jax.ShapeDtypeStruct((B, L, KV_WIDTH), BF16),
        jax.ShapeDtypeStruct((B, ATT_KV_HEADS, 2 * HEAD_DIM, L), BF16),
        hg, hg, hg, hg, hg, hg, hg,
        jax.ShapeDtypeStruct((B, L, CONV_WIDTH), F32),
    )
    out_specs = (
        pl.BlockSpec((None, ATT_WIDTH, tm), lambda b, i: (b, 0, i)),
        tok(KV_WIDTH),
        pl.BlockSpec((None, ATT_KV_HEADS, 2 * HEAD_DIM, tm), lambda b, i: (b, 0, 0, i)),
        tok(HGRN_WIDTH), tok(HGRN_WIDTH), tok(HGRN_WIDTH), tok(HGRN_WIDTH), tok(HGRN_WIDTH), tok(HGRN_WIDTH),
        tok(HGRN_WIDTH), tok(CONV_WIDTH),
    )
    in_specs = [
        tok(D), full(g), full(w), full(qn), full(kn),
        pl.BlockSpec((tm, LANES), lambda b, i: (i, 0)),
        pl.BlockSpec((tm, LANES), lambda b, i: (i, 0)),
        full(lbf), full(lbb),
    ]
    return pl.pallas_call(
        functools.partial(_in_proj_kernel, layer=layer),
        grid=(B, nt),
        in_specs=in_specs,
        out_specs=out_specs,
        out_shape=out_shape,
        compiler_params=pltpu.CompilerParams(
            dimension_semantics=("parallel", "parallel"), vmem_limit_bytes=VMEM_LIMIT),
        name=f"in_proj_{layer}",
    )(x, g, w, qn, kn, cos, sin, lbf, lbb)


def _attn_kernel(qt_ref, k_ref, vat_ref, o_ref, qs_ref, s_ref, smax_ref, m_ref, acc_ref, *, tq, tk, nk):
    h = pl.program_id(1)
    qt = qt_ref[...]
    q4 = jnp.concatenate(
        [qt[g * HEAD_DIM:(g + 1) * HEAD_DIM, :] for g in range(ATT_GROUP)], axis=1)
    row_head = lax.broadcasted_iota(jnp.int32, (KV_WIDTH, 1), 0) // HEAD_DIM
    qs_ref[...] = jnp.where(row_head == h, jnp.concatenate([q4] * ATT_KV_HEADS, axis=0), jnp.zeros((), BF16))
    m_ref[...] = jnp.full(m_ref.shape, -jnp.inf, F32)
    acc_ref[...] = jnp.zeros(acc_ref.shape, F32)

    def scores(j, g, slot):
        k0 = pl.multiple_of(j * tk, tk)
        cols = slice(g * tq, (g + 1) * tq)
        s = _dot(k_ref[pl.ds(k0, tk), :], qs_ref[:, cols])
        s_ref[:, cols] = s
        smax_ref[slot, :, cols] = jnp.max(s, axis=0, keepdims=True)

    def step(j, slot, last):
        k0 = pl.multiple_of(j * tk, tk)
        m_prev = m_ref[...]
        m_new = jnp.maximum(m_prev, smax_ref[slot])
        alpha = jnp.exp2(m_prev - m_new)
        vat = vat_ref[:, pl.ds(k0, tk)]
        for g in range(ATT_GROUP):
            cols = slice(g * tq, (g + 1) * tq)
            p = jnp.exp2(s_ref[:, cols] - m_new[:, cols]).astype(BF16)
            acc_ref[:, cols] = alpha[:, cols] * acc_ref[:, cols] + _dot(vat, p)
            if not last:
                scores(j + 1, g, 1 - slot)
        m_ref[...] = m_new

    for g in range(ATT_GROUP):
        scores(0, g, 0)

    def body(jj, carry):
        for u in range(KV_UNROLL):
            step(KV_UNROLL * jj + u, u % 2, False)
        return carry

    lax.fori_loop(0, nk // KV_UNROLL - 1, body, 0)
    for u in range(KV_UNROLL):
        step(nk - KV_UNROLL + u, u % 2, u == KV_UNROLL - 1)
    acc = acc_ref[...]
    res = acc[:HEAD_DIM, :] / acc[HEAD_DIM:, :]
    o_ref[...] = jnp.concatenate(
        [jnp.transpose(res[:, g * tq:(g + 1) * tq]) for g in range(ATT_GROUP)], axis=1).astype(o_ref.dtype)


def _attention(qt, k, vat, *, tq, tk):
    B, _, L = qt.shape
    gw = ATT_GROUP * HEAD_DIM
    mq = ATT_GROUP * tq
    assert (L // tk) % KV_UNROLL == 0, "the kv loop handles KV_UNROLL chunks per trip"
    return pl.pallas_call(
        functools.partial(_attn_kernel, tq=tq, tk=tk, nk=L // tk),
        grid=(B, ATT_KV_HEADS, L // tq),
        in_specs=[
            pl.BlockSpec((None, gw, tq), lambda b, h, i: (b, h, i)),
            pl.BlockSpec((None, L, KV_WIDTH), lambda b, h, i: (b, 0, 0)),
            pl.BlockSpec((None, None, 2 * HEAD_DIM, L), lambda b, h, i: (b, h, 0, 0)),
        ],
        out_specs=pl.BlockSpec((None, tq, gw), lambda b, h, i: (b, i, h)),
        out_shape=jax.ShapeDtypeStruct((B, L, ATT_WIDTH), BF16),
        scratch_shapes=[
            pltpu.VMEM((KV_WIDTH, mq), BF16),
            pltpu.VMEM((tk, mq), F32),
            pltpu.VMEM((2, 1, mq), F32),
            pltpu.VMEM((1, mq), F32),
            pltpu.VMEM((2 * HEAD_DIM, mq), F32),
        ],
        compiler_params=pltpu.CompilerParams(
            dimension_semantics=("parallel", "parallel", "parallel"), vmem_limit_bytes=VMEM_LIMIT),
        name="attention",
    )(qt, k, vat)


def _hgrn_kernel(q_ref, v_ref, lf_ref, k_ref, o_ref, s_ref, *, T, reverse):
    C = HGRN_CHUNK
    W = HGRN_WIDTH

    @pl.when(pl.program_id(1) == 0)
    def _():
        s_ref[...] = jnp.zeros(s_ref.shape, F32)

    q = q_ref[...]
    v = v_ref[...]
    k = k_ref[...]
    lf = lf_ref[...]

    r = lax.broadcasted_iota(jnp.int32, (T, T), 0)
    c = lax.broadcasted_iota(jnp.int32, (T, T), 1)
    same = (r // C) == (c // C)
    tri = same & ((c >= r) if reverse else (c <= r))
    m_tri = jnp.where(tri, 1.0, 0.0).astype(BF16)
    m_all = jnp.where(same, 1.0, 0.0).astype(BF16)

    def left_split_dot(m, x):
        acc = None
        rem = x
        for _ in range(3):
            t = rem.astype(BF16)
            acc = _dot(m, t) if acc is None else acc + _dot(m, t)
            rem = rem - t.astype(F32)
        return acc

    b = left_split_dot(m_tri, lf * LOG2E)
    bl = left_split_dot(m_all, lf * LOG2E)

    zpad = jnp.zeros((SUBLANES, W), F32)
    padded = [jnp.concatenate([a, zpad] if reverse else [zpad, a], axis=0) for a in (k, b, v)]
    pos = lax.broadcasted_iota(jnp.int32, (T, 1), 0) % C
    head_sum = _group_matrix(W, W // HGRN_HEADS, 1.0)
    acc = jnp.zeros((T, W), F32)
    for r in range(SUBLANES):
        if r == 0:
            shifted = padded
        else:
            shifted = [pltpu.roll(a, (T + SUBLANES - r) if reverse else r, 0) for a in padded]
        for d in range(r, C, SUBLANES):
            near = (d == r) != reverse
            lo = SUBLANES if near else 0
            ks, bs, vs = (a[lo:lo + T, :] for a in shifted)
            valid = (pos <= C - 1 - d) if reverse else (pos >= d)
            w = jnp.where(valid, q * ks * jnp.exp2(b - bs), 0.0)
            acc = acc + _dot(w.astype(BF16), head_sum) * vs

    qt = (q * jnp.exp2(b)).astype(BF16)
    kt = (k * jnp.exp2(bl - b)).astype(BF16)
    dec = jnp.exp2(bl)
    vb = v.astype(BF16)
    dh = W // HGRN_HEADS
    lane = lax.broadcasted_iota(jnp.int32, (1, W), 1)
    hmask = [(lane // dh) == hh for hh in range(HGRN_HEADS)]
    zero = jnp.zeros((), BF16)

    s = s_ref[...]
    order = range(T // C - 1, -1, -1) if reverse else range(T // C)
    for n in order:
        r0 = n * C
        qn = qt[r0:r0 + C]
        kn = kt[r0:r0 + C]
        vn = vb[r0:r0 + C]
        qexp = jnp.concatenate([jnp.where(hmask[hh], qn, zero) for hh in range(HGRN_HEADS)], axis=0)
        o4 = lax.dot_general(qexp, s.astype(BF16), (((1,), (1,)), ((), ())), preferred_element_type=F32)
        on = jnp.concatenate([o4[hh * C:(hh + 1) * C, :] for hh in range(HGRN_HEADS)], axis=1)
        o_ref[r0:r0 + C, :] = acc[r0:r0 + C] + on
        kexp = jnp.concatenate([jnp.where(hmask[hh], kn, zero) for hh in range(HGRN_HEADS)], axis=0)
        vexp = jnp.concatenate([vn[:, hh * dh:(hh + 1) * dh] for hh in range(HGRN_HEADS)], axis=0)
        u = lax.dot_general(vexp, kexp, (((0,), (0,)), ((), ())), preferred_element_type=F32)
        s = s * dec[r0:r0 + 1, :] + u
    s_ref[...] = s


def _hgrn(q, v, lf, k, *, T, reverse):
    B, L, W = q.shape
    nt = L // T
    idx = (lambda b, i: (b, nt - 1 - i, 0)) if reverse else (lambda b, i: (b, i, 0))
    spec = pl.BlockSpec((None, T, W), idx)
    return pl.pallas_call(
        functools.partial(_hgrn_kernel, T=T, reverse=reverse),
        grid=(B, nt),
        in_specs=[spec, spec, spec, spec],
        out_specs=spec,
        out_shape=jax.ShapeDtypeStruct((B, L, W), F32),
        scratch_shapes=[pltpu.VMEM((W // HGRN_HEADS, W), F32)],
        compiler_params=pltpu.CompilerParams(
            dimension_semantics=("parallel", "arbitrary"), vmem_limit_bytes=VMEM_LIMIT),
        name="hgrn_bwd" if reverse else "hgrn_fwd",
    )(q, v, lf, k)


def _out_mlp_kernel(x_ref, oa_ref, ofw_ref, obw_ref, sg_ref, hn_ref, up_ref, uc_ref, un_ref,
                    cw_ref, cb_ref, lg_ref, lb_ref, wo_ref, nm_ref, w1_ref, w2_ref, out_ref, ext_ref,
                    *, tm, ff_chunk):
    i = pl.program_id(1)
    nt = pl.num_programs(1)

    o = ofw_ref[...] + obw_ref[...]
    gmat = _group_mean_matrix(HGRN_WIDTH, HGRN_WIDTH // HGRN_HEADS)
    o_hg = o * lax.rsqrt(_split_dot(o * o, gmat, 2) + EPS) * hn_ref[...] * sg_ref[...]

    ext_ref[0:CONV_HALO, :] = jnp.where(i > 0, up_ref[...], 0.0)
    ext_ref[CONV_HALO:CONV_HALO + tm, :] = uc_ref[...]
    ext_ref[CONV_HALO + tm:, :] = jnp.where(i < nt - 1, un_ref[...], 0.0)
    y = jnp.zeros((tm, CONV_WIDTH), F32)
    for w in range(CONV_KERNEL):
        y = y + ext_ref[pl.ds(CONV_HALO - CONV_PAD + w, tm), :] * cw_ref[w:w + 1, :]
    y = y + cb_ref[...]
    mu = jnp.mean(y, axis=-1, keepdims=True)
    yc = y - mu
    yn = yc * lax.rsqrt(jnp.mean(yc * yc, axis=-1, keepdims=True) + EPS) * lg_ref[...] + lb_ref[...]
    o_cv = yn * _sigmoid(yn)

    mix = (_dot(oa_ref[...], wo_ref[0:ATT_WIDTH, :])
           + _dot(o_hg.astype(BF16), wo_ref[ATT_WIDTH:ATT_WIDTH + HGRN_WIDTH, :])
           + _dot(o_cv.astype(BF16), wo_ref[ATT_WIDTH + HGRN_WIDTH:, :]))
    x1 = x_ref[...] + mix

    ms = jnp.mean(x1 * x1, axis=-1, keepdims=True)
    hn = (x1 * lax.rsqrt(ms + EPS) * nm_ref[...]).astype(BF16)
    mlp = None
    d_ff = w1_ref.shape[1]
    for c0 in range(0, d_ff, ff_chunk):
        a = jnp.maximum(_dot(hn, w1_ref[:, c0:c0 + ff_chunk]), 0.0)
        part = _dot((a * a).astype(BF16), w2_ref[c0:c0 + ff_chunk, :])
        mlp = part if mlp is None else mlp + part
    out_ref[...] = x1 + mlp


def _out_mlp(x, oa, ofw, obw, sg, hn_g, u, cw, cb, lg, lb, wo, nm, w1, w2, *, tm):
    B, L, D = x.shape
    nt = L // tm
    hb = tm // CONV_HALO
    nhb = L // CONV_HALO
    tok = lambda width: pl.BlockSpec((None, tm, width), lambda b, i: (b, i, 0))
    const = lambda a: pl.BlockSpec(a.shape, lambda b, i: (0,) * a.ndim)
    in_specs = [
        tok(D), tok(ATT_WIDTH), tok(HGRN_WIDTH), tok(HGRN_WIDTH), tok(HGRN_WIDTH), const(hn_g),
        pl.BlockSpec((None, CONV_HALO, CONV_WIDTH), lambda b, i: (b, jnp.maximum(i * hb - 1, 0), 0)),
        tok(CONV_WIDTH),
        pl.BlockSpec((None, CONV_HALO, CONV_WIDTH), lambda b, i: (b, jnp.minimum((i + 1) * hb, nhb - 1), 0)),
        const(cw), const(cb), const(lg), const(lb), const(wo), const(nm), const(w1), const(w2),
    ]
    return pl.pallas_call(
        functools.partial(_out_mlp_kernel, tm=tm, ff_chunk=min(1024, w1.shape[1])),
        grid=(B, nt),
        in_specs=in_specs,
        out_specs=tok(D),
        out_shape=jax.ShapeDtypeStruct((B, L, D), F32),
        scratch_shapes=[pltpu.VMEM((tm + 2 * CONV_HALO, CONV_WIDTH), F32)],
        compiler_params=pltpu.CompilerParams(
            dimension_semantics=("parallel", "parallel"), vmem_limit_bytes=VMEM_LIMIT),
        name="out_mlp",
    )(x, oa, ofw, obw, sg, hn_g, u, u, u, cw, cb, lg, lb, wo, nm, w1, w2)


def _rope_tables(L):
    rows = L // GRID_W
    row_idx = jnp.repeat(jnp.arange(rows), GRID_W)
    col_idx = jnp.tile(jnp.arange(GRID_W), rows)
    inv_freq = ROPE_THETA ** (-jnp.arange(0, ROPE_AXIS_DIM, 2, dtype=F32) / ROPE_AXIS_DIM)
    ar = row_idx.astype(F32)[:, None] * inv_freq[None, :]
    ac = col_idx.astype(F32)[:, None] * inv_freq[None, :]
    cos = jnp.concatenate([jnp.cos(ar), jnp.cos(ar), jnp.cos(ac), jnp.cos(ac)], axis=-1)
    sin = jnp.concatenate([-jnp.sin(ar), jnp.sin(ar), -jnp.sin(ac), jnp.sin(ac)], axis=-1)
    reps = LANES // HEAD_DIM
    return jnp.tile(cos, (1, reps)), jnp.tile(sin, (1, reps))


def _tile(L, want):
    t = min(want, L)
    assert L % t == 0, (L, t)
    return t


def kernel(x, w_in, w_out, norm_mix, norm_mlp, q_norm, k_norm, hgrn_lb_fwd, hgrn_lb_bwd, hgrn_norm,
           conv_w, conv_b, conv_ln_g, conv_ln_b, w_mlp_in, w_mlp_out):
    B, L, D = x.shape
    depth = w_in.shape[0]
    assert w_in.shape[2] == IN_COLS and L % GRID_W == 0
    cos, sin = _rope_tables(L)
    row = lambda a: a.reshape(1, -1).astype(F32)
    reps = LANES // HEAD_DIM
    tm_in = _tile(L, 512)
    tq = _tile(L, 512)
    tk = _tile(L // KV_UNROLL, 512)
    t_hg = _tile(L, 256)
    tm_out = _tile(L, 512)

    for l in range(depth):
        (q, kt, va, hq, hv, lff, kf, lfb, kb, sg, u) = _in_proj(
            x, row(norm_mix[l]), w_in[l].astype(BF16),
            jnp.tile(row(q_norm[l]), (1, reps)), jnp.tile(row(k_norm[l]), (1, reps)),
            cos, sin, hgrn_lb_fwd.astype(F32), hgrn_lb_bwd.astype(F32), layer=l, tm=tm_in)
        o_att = _attention(q, kt, va, tq=tq, tk=tk)
        o_fw = _hgrn(hq, hv, lff, kf, T=t_hg, reverse=False)
        o_bw = _hgrn(hq, hv, lfb, kb, T=t_hg, reverse=True)
        x = _out_mlp(
            x, o_att, o_fw, o_bw, sg, row(hgrn_norm[l]), u,
            conv_w[l].astype(F32), row(conv_b[l]), row(conv_ln_g[l]), row(conv_ln_b[l]),
            w_out[l].astype(BF16), row(norm_mlp[l]), w_mlp_in[l].astype(BF16), w_mlp_out[l].astype(BF16),
            tm=tm_out)
    return x
```

```python
import functools

import jax
import jax.numpy as jnp
from jax import lax
from jax.experimental import pallas as pl
from jax.experimental.pallas import tpu as pltpu

F32 = jnp.float32
BF16 = jnp.bfloat16

EPS = 1e-6
GRID_W = 64
HEAD_DIM = 64
ATT_HEADS = 8
ATT_KV_HEADS = 2
ATT_GROUP = ATT_HEADS // ATT_KV_HEADS
ATT_WIDTH = ATT_HEADS * HEAD_DIM
KV_WIDTH = ATT_KV_HEADS * HEAD_DIM
ROPE_THETA = 10000.0
ROPE_AXIS_DIM = HEAD_DIM // 2
HGRN_WIDTH = 256
HGRN_HEADS = 4
HGRN_CHUNK = 16
CONV_WIDTH = 256
CONV_KERNEL = 31
CONV_PAD = (CONV_KERNEL - 1) // 2
CONV_HALO = 16
KV_UNROLL = 8

LANES = 128
SUBLANES = 8
LOG2E = 1.4426950408889634
VMEM_LIMIT = 56 * 1024 * 1024

_C_Q = 0
_C_K = _C_Q + ATT_WIDTH
_C_V = _C_K + KV_WIDTH
_C_HQ = _C_V + KV_WIDTH
_C_HI = _C_HQ + HGRN_WIDTH
_C_FF = _C_HI + HGRN_WIDTH
_C_FB = _C_FF + HGRN_WIDTH
_C_HG = _C_FB + HGRN_WIDTH
_C_CA = _C_HG + HGRN_WIDTH
_C_CG = _C_CA + CONV_WIDTH
IN_COLS = _C_CG + CONV_WIDTH


def _dot(a, b):
    return jnp.dot(a, b, preferred_element_type=F32)


def _split_dot(x, m, parts):
    acc = None
    r = x
    for _ in range(parts):
        t = r.astype(BF16)
        acc = _dot(t, m) if acc is None else acc + _dot(t, m)
        r = r - t.astype(F32)
    return acc


def _group_matrix(width, group, value):
    r = lax.broadcasted_iota(jnp.int32, (width, width), 0) // group
    c = lax.broadcasted_iota(jnp.int32, (width, width), 1) // group
    return jnp.where(r == c, value, 0.0).astype(BF16)


def _group_mean_matrix(width, group):
    return _group_matrix(width, group, 1.0 / group)


def _sigmoid(x):
    return 1.0 / (1.0 + jnp.exp(-x))


def _in_proj_kernel(x_ref, g_ref, w_ref, qn_ref, kn_ref, cos_ref, sin_ref, lbf_ref, lbb_ref,
                    qt_out, k_out, vat_out, hq_out, hv_out, lff_out, kf_out, lfb_out, kb_out, sg_out, u_out,
                    *, layer):
    x = x_ref[...]
    ms = jnp.mean(x * x, axis=-1, keepdims=True)
    h = (x * lax.rsqrt(ms + EPS) * g_ref[...]).astype(BF16)

    def proj(lo, width):
        return _dot(h, w_ref[:, lo:lo + width])

    gmat = _group_mean_matrix(LANES, HEAD_DIM)
    cos = cos_ref[...]
    sin = sin_ref[...]
    lane = lax.broadcasted_iota(jnp.int32, (1, LANES), 1)
    first_half = (lane % ROPE_AXIS_DIM) < (ROPE_AXIS_DIM // 2)

    def norm_rope(z, gain, scale):
        zn = z * lax.rsqrt(_split_dot(z * z, gmat, 2) + EPS) * gain
        half = ROPE_AXIS_DIM // 2
        partner = jnp.where(first_half, pltpu.roll(zn, LANES - half, 1), pltpu.roll(zn, half, 1))
        return (zn * cos + partner * sin) * scale

    zq = proj(_C_Q, ATT_WIDTH)
    q_scale = HEAD_DIM ** -0.5 * LOG2E
    for c in range(ATT_WIDTH // LANES):
        qc = norm_rope(zq[:, c * LANES:(c + 1) * LANES], qn_ref[...], q_scale)
        qt_out[c * LANES:(c + 1) * LANES, :] = jnp.transpose(qc).astype(BF16)

    k_out[...] = norm_rope(proj(_C_K, KV_WIDTH), kn_ref[...], 1.0).astype(BF16)

    zvt = jnp.transpose(proj(_C_V, KV_WIDTH)).astype(BF16)
    ones = jnp.ones((HEAD_DIM, zvt.shape[1]), BF16)
    for hh in range(ATT_KV_HEADS):
        vat_out[hh] = jnp.concatenate([zvt[hh * HEAD_DIM:(hh + 1) * HEAD_DIM, :], ones], axis=0)

    def lower_bound(lb_ref):
        p = lb_ref[...]
        p = jnp.exp(p - jnp.max(p, axis=0, keepdims=True))
        p = p / jnp.sum(p, axis=0, keepdims=True)
        lb = jnp.zeros((1, HGRN_WIDTH), F32)
        for i in range(1, layer + 1):
            lb = lb + p[i:i + 1, :]
        return lb

    def forget(z, lb):
        t = jnp.exp(-z)
        sig = 1.0 / (1.0 + t)
        return jnp.log(lb + (1.0 - lb) * sig), (1.0 - lb) * (t * sig)

    hq_out[...] = proj(_C_HQ, HGRN_WIDTH)
    hv_out[...] = proj(_C_HI, HGRN_WIDTH)
    lf, kk = forget(proj(_C_FF, HGRN_WIDTH), lower_bound(lbf_ref))
    lff_out[...] = lf
    kf_out[...] = kk
    lf, kk = forget(proj(_C_FB, HGRN_WIDTH), lower_bound(lbb_ref))
    lfb_out[...] = lf
    kb_out[...] = kk
    zg = proj(_C_HG, HGRN_WIDTH)
    sg_out[...] = zg * _sigmoid(zg)

    u_out[...] = proj(_C_CA, CONV_WIDTH) * _sigmoid(proj(_C_CG, CONV_WIDTH))


def _in_proj(x, g, w, qn, kn, cos, sin, lbf, lbb, *, layer, tm):
    B, L, D = x.shape
    nt = L // tm
    tok = lambda width: pl.BlockSpec((None, tm, width), lambda b, i: (b, i, 0))
    full = lambda a: pl.BlockSpec(a.shape, lambda b, i: (0,) * a.ndim)
    hg = jax.ShapeDtypeStruct((B, L, HGRN_WIDTH), F32)
    out_shape = (
        jax.ShapeDtypeStruct((B, ATT_WIDTH, L), BF16),
        jax.ShapeDtypeStruct((B, L, KV_WIDTH), BF16),
        jax.ShapeDtypeStruct((B, ATT_KV_HEADS, 2 * HEAD_DIM, L), BF16),
        hg, hg, hg, hg, hg, hg, hg,
        jax.ShapeDtypeStruct((B, L, CONV_WIDTH), F32),
    )
    out_specs = (
        pl.BlockSpec((None, ATT_WIDTH, tm), lambda b, i: (b, 0, i)),
        tok(KV_WIDTH),
        pl.BlockSpec((None, ATT_KV_HEADS, 2 * HEAD_DIM, tm), lambda b, i: (b, 0, 0, i)),
        tok(HGRN_WIDTH), tok(HGRN_WIDTH), tok(HGRN_WIDTH), tok(HGRN_WIDTH), tok(HGRN_WIDTH), tok(HGRN_WIDTH),
        tok(HGRN_WIDTH), tok(CONV_WIDTH),
    )
    in_specs = [
        tok(D), full(g), full(w), full(qn), full(kn),
        pl.BlockSpec((tm, LANES), lambda b, i: (i, 0)),
        pl.BlockSpec((tm, LANES), lambda b, i: (i, 0)),
        full(lbf), full(lbb),
    ]
    return pl.pallas_call(
        functools.partial(_in_proj_kernel, layer=layer),
        grid=(B, nt),
        in_specs=in_specs,
        out_specs=out_specs,
        out_shape=out_shape,
        compiler_params=pltpu.CompilerParams(
            dimension_semantics=("parallel", "parallel"), vmem_limit_bytes=VMEM_LIMIT),
        name=f"in_proj_{layer}",
    )(x, g, w, qn, kn, cos, sin, lbf, lbb)


def _attn_kernel(qt_ref, k_ref, vat_ref, o_ref, qs_ref, s_ref, smax_ref, m_ref, acc_ref, *, tq, tk, nk):
    h = pl.program_id(1)
    qt = qt_ref[...]
    q4 = jnp.concatenate(
        [qt[g * HEAD_DIM:(g + 1) * HEAD_DIM, :] for g in range(ATT_GROUP)], axis=1)
    row_head = lax.broadcasted_iota(jnp.int32, (KV_WIDTH, 1), 0) // HEAD_DIM
    qs_ref[...] = jnp.where(row_head == h, jnp.concatenate([q4] * ATT_KV_HEADS, axis=0), jnp.zeros((), BF16))
    m_ref[...] = jnp.full(m_ref.shape, -jnp.inf, F32)
    acc_ref[...] = jnp.zeros(acc_ref.shape, F32)

    def scores(j, g, slot):
        k0 = pl.multiple_of(j * tk, tk)
        cols = slice(g * tq, (g + 1) * tq)
        s = _dot(k_ref[pl.ds(k0, tk), :], qs_ref[:, cols])
        s_ref[:, cols] = s
        smax_ref[slot, :, cols] = jnp.max(s, axis=0, keepdims=True)

    def step(j, slot, last):
        k0 = pl.multiple_of(j * tk, tk)
        m_prev = m_ref[...]
        m_new = jnp.maximum(m_prev, smax_ref[slot])
        alpha = jnp.exp2(m_prev - m_new)
        vat = vat_ref[:, pl.ds(k0, tk)]
        for g in range(ATT_GROUP):
            cols = slice(g * tq, (g + 1) * tq)
            p = jnp.exp2(s_ref[:, cols] - m_new[:, cols]).astype(BF16)
            acc_ref[:, cols] = alpha[:, cols] * acc_ref[:, cols] + _dot(vat, p)
            if not last:
                scores(j + 1, g, 1 - slot)
        m_ref[...] = m_new

    for g in range(ATT_GROUP):
        scores(0, g, 0)

    def body(jj, carry):
        for u in range(KV_UNROLL):
            step(KV_UNROLL * jj + u, u % 2, False)
        return carry

    lax.fori_loop(0, nk // KV_UNROLL - 1, body, 0)
    for u in range(KV_UNROLL):
        step(nk - KV_UNROLL + u, u % 2, u == KV_UNROLL - 1)
    acc = acc_ref[...]
    res = acc[:HEAD_DIM, :] / acc[HEAD_DIM:, :]
    o_ref[...] = jnp.concatenate(
        [jnp.transpose(res[:, g * tq:(g + 1) * tq]) for g in range(ATT_GROUP)], axis=1).astype(o_ref.dtype)


def _attention(qt, k, vat, *, tq, tk):
    B, _, L = qt.shape
    gw = ATT_GROUP * HEAD_DIM
    mq = ATT_GROUP * tq
    assert (L // tk) % KV_UNROLL == 0, "the kv loop handles KV_UNROLL chunks per trip"
    return pl.pallas_call(
        functools.partial(_attn_kernel, tq=tq, tk=tk, nk=L // tk),
        grid=(B, ATT_KV_HEADS, L // tq),
        in_specs=[
            pl.BlockSpec((None, gw, tq), lambda b, h, i: (b, h, i)),
            pl.BlockSpec((None, L, KV_WIDTH), lambda b, h, i: (b, 0, 0)),
            pl.BlockSpec((None, None, 2 * HEAD_DIM, L), lambda b, h, i: (b, h, 0, 0)),
        ],
        out_specs=pl.BlockSpec((None, tq, gw), lambda b, h, i: (b, i, h)),
        out_shape=jax.ShapeDtypeStruct((B, L, ATT_WIDTH), BF16),
        scratch_shapes=[
            pltpu.VMEM((KV_WIDTH, mq), BF16),
            pltpu.VMEM((tk, mq), F32),
            pltpu.VMEM((2, 1, mq), F32),
            pltpu.VMEM((1, mq), F32),
            pltpu.VMEM((2 * HEAD_DIM, mq), F32),
        ],
        compiler_params=pltpu.CompilerParams(
            dimension_semantics=("parallel", "parallel", "parallel"), vmem_limit_bytes=VMEM_LIMIT),
        name="attention",
    )(qt, k, vat)


def _hgrn_kernel(q_ref, v_ref, lf_ref, k_ref, o_ref, s_ref, *, T, reverse):
    C = HGRN_CHUNK
    W = HGRN_WIDTH

    @pl.when(pl.program_id(1) == 0)
    def _():
        s_ref[...] = jnp.zeros(s_ref.shape, F32)

    q = q_ref[...]
    v = v_ref[...]
    k = k_ref[...]
    lf = lf_ref[...]

    r = lax.broadcasted_iota(jnp.int32, (T, T), 0)
    c = lax.broadcasted_iota(jnp.int32, (T, T), 1)
    same = (r // C) == (c // C)
    tri = same & ((c >= r) if reverse else (c <= r))
    m_tri = jnp.where(tri, 1.0, 0.0).astype(BF16)
    m_all = jnp.where(same, 1.0, 0.0).astype(BF16)

    def left_split_dot(m, x):
        acc = None
        rem = x
        for _ in range(3):
            t = rem.astype(BF16)
            acc = _dot(m, t) if acc is None else acc + _dot(m, t)
            rem = rem - t.astype(F32)
        return acc

    b = left_split_dot(m_tri, lf * LOG2E)
    bl = left_split_dot(m_all, lf * LOG2E)

    zpad = jnp.zeros((SUBLANES, W), F32)
    padded = [jnp.concatenate([a, zpad] if reverse else [zpad, a], axis=0) for a in (k, b, v)]
    pos = lax.broadcasted_iota(jnp.int32, (T, 1), 0) % C
    head_sum = _group_matrix(W, W // HGRN_HEADS, 1.0)
    acc = jnp.zeros((T, W), F32)
    for r in range(SUBLANES):
        if r == 0:
            shifted = padded
        else:
            shifted = [pltpu.roll(a, (T + SUBLANES - r) if reverse else r, 0) for a in padded]
        for d in range(r, C, SUBLANES):
            near = (d == r) != reverse
            lo = SUBLANES if near else 0
            ks, bs, vs = (a[lo:lo + T, :] for a in shifted)
            valid = (pos <= C - 1 - d) if reverse else (pos >= d)
            w = jnp.where(valid, q * ks * jnp.exp2(b - bs), 0.0)
            acc = acc + _dot(w.astype(BF16), head_sum) * vs

    qt = (q * jnp.exp2(b)).astype(BF16)
    kt = (k * jnp.exp2(bl - b)).astype(BF16)
    dec = jnp.exp2(bl)
    vb = v.astype(BF16)
    dh = W // HGRN_HEADS
    lane = lax.broadcasted_iota(jnp.int32, (1, W), 1)
    hmask = [(lane // dh) == hh for hh in range(HGRN_HEADS)]
    zero = jnp.zeros((), BF16)

    s = s_ref[...]
    order = range(T // C - 1, -1, -1) if reverse else range(T // C)
    for n in order:
        r0 = n * C
        qn = qt[r0:r0 + C]
        kn = kt[r0:r0 + C]
        vn = vb[r0:r0 + C]
        qexp = jnp.concatenate([jnp.where(hmask[hh], qn, zero) for hh in range(HGRN_HEADS)], axis=0)
        o4 = lax.dot_general(qexp, s.astype(BF16), (((1,), (1,)), ((), ())), preferred_element_type=F32)
        on = jnp.concatenate([o4[hh * C:(hh + 1) * C, :] for hh in range(HGRN_HEADS)], axis=1)
        o_ref[r0:r0 + C, :] = acc[r0:r0 + C] + on
        kexp = jnp.concatenate([jnp.where(hmask[hh], kn, zero) for hh in range(HGRN_HEADS)], axis=0)
        vexp = jnp.concatenate([vn[:, hh * dh:(hh + 1) * dh] for hh in range(HGRN_HEADS)], axis=0)
        u = lax.dot_general(vexp, kexp, (((0,), (0,)), ((), ())), preferred_element_type=F32)
        s = s * dec[r0:r0 + 1, :] + u
    s_ref[...] = s


def _hgrn(q, v, lf, k, *, T, reverse):
    B, L, W = q.shape
    nt = L // T
    idx = (lambda b, i: (b, nt - 1 - i, 0)) if reverse else (lambda b, i: (b, i, 0))
    spec = pl.BlockSpec((None, T, W), idx)
    return pl.pallas_call(
        functools.partial(_hgrn_kernel, T=T, reverse=reverse),
        grid=(B, nt),
        in_specs=[spec, spec, spec, spec],
        out_specs=spec,
        out_shape=jax.ShapeDtypeStruct((B, L, W), F32),
        scratch_shapes=[pltpu.VMEM((W // HGRN_HEADS, W), F32)],
        compiler_params=pltpu.CompilerParams(
            dimension_semantics=("parallel", "arbitrary"), vmem_limit_bytes=VMEM_LIMIT),
        name="hgrn_bwd" if reverse else "hgrn_fwd",
    )(q, v, lf, k)


def _out_mlp_kernel(x_ref, oa_ref, ofw_ref, obw_ref, sg_ref, hn_ref, up_ref, uc_ref, un_ref,
                    cw_ref, cb_ref, lg_ref, lb_ref, wo_ref, nm_ref, w1_ref, w2_ref, out_ref, ext_ref, stage_ref,
                    *, tm, sub, ff_chunk):
    i = pl.program_id(1)
    nt = pl.num_programs(1)

    ext_ref[0:CONV_HALO, :] = jnp.where(i > 0, up_ref[...], 0.0)
    ext_ref[CONV_HALO:CONV_HALO + tm, :] = uc_ref[...]
    ext_ref[CONV_HALO + tm:, :] = jnp.where(i < nt - 1, un_ref[...], 0.0)
    gmat = _group_mean_matrix(HGRN_WIDTH, HGRN_WIDTH // HGRN_HEADS)

    def mixer_inputs(r0):
        rows = slice(r0, r0 + sub)
        o = ofw_ref[rows, :] + obw_ref[rows, :]
        o_hg = o * lax.rsqrt(_split_dot(o * o, gmat, 2) + EPS) * hn_ref[...] * sg_ref[rows, :]

        y = jnp.zeros((sub, CONV_WIDTH), F32)
        for w in range(CONV_KERNEL):
            y = y + ext_ref[pl.ds(r0 + CONV_HALO - CONV_PAD + w, sub), :] * cw_ref[w:w + 1, :]
        y = y + cb_ref[...]
        mu = jnp.mean(y, axis=-1, keepdims=True)
        yc = y - mu
        yn = yc * lax.rsqrt(jnp.mean(yc * yc, axis=-1, keepdims=True) + EPS) * lg_ref[...] + lb_ref[...]
        o_cv = yn * _sigmoid(yn)
        stage_ref[rows, 0:HGRN_WIDTH] = o_hg.astype(BF16)
        stage_ref[rows, HGRN_WIDTH:] = o_cv.astype(BF16)

    def project_mlp(r0):
        rows = slice(r0, r0 + sub)
        mix = (_dot(oa_ref[rows, :], wo_ref[0:ATT_WIDTH, :])
               + _dot(stage_ref[rows, :], wo_ref[ATT_WIDTH:, :]))
        x1 = x_ref[rows, :] + mix

        ms = jnp.mean(x1 * x1, axis=-1, keepdims=True)
        hn = (x1 * lax.rsqrt(ms + EPS) * nm_ref[...]).astype(BF16)
        mlp = None
        d_ff = w1_ref.shape[1]
        for c0 in range(0, d_ff, ff_chunk):
            a = jnp.maximum(_dot(hn, w1_ref[:, c0:c0 + ff_chunk]), 0.0)
            part = _dot((a * a).astype(BF16), w2_ref[c0:c0 + ff_chunk, :])
            mlp = part if mlp is None else mlp + part
        out_ref[rows, :] = x1 + mlp

    for r0 in range(0, tm, sub):
        mixer_inputs(r0)
    for r0 in range(0, tm, sub):
        project_mlp(r0)


def _out_mlp(x, oa, ofw, obw, sg, hn_g, u, cw, cb, lg, lb, wo, nm, w1, w2, *, tm):
    B, L, D = x.shape
    nt = L // tm
    hb = tm // CONV_HALO
    nhb = L // CONV_HALO
    tok = lambda width: pl.BlockSpec((None, tm, width), lambda b, i: (b, i, 0))
    const = lambda a: pl.BlockSpec(a.shape, lambda b, i: (0,) * a.ndim, pipeline_mode=pl.Buffered(1))
    in_specs = [
        tok(D), tok(ATT_WIDTH), tok(HGRN_WIDTH), tok(HGRN_WIDTH), tok(HGRN_WIDTH), const(hn_g),
        pl.BlockSpec((None, CONV_HALO, CONV_WIDTH), lambda b, i: (b, jnp.maximum(i * hb - 1, 0), 0)),
        tok(CONV_WIDTH),
        pl.BlockSpec((None, CONV_HALO, CONV_WIDTH), lambda b, i: (b, jnp.minimum((i + 1) * hb, nhb - 1), 0)),
        const(cw), const(cb), const(lg), const(lb), const(wo), const(nm), const(w1), const(w2),
    ]
    return pl.pallas_call(
        functools.partial(_out_mlp_kernel, tm=tm, sub=min(512, tm), ff_chunk=min(1024, w1.shape[1])),
        grid=(B, nt),
        in_specs=in_specs,
        out_specs=tok(D),
        out_shape=jax.ShapeDtypeStruct((B, L, D), F32),
        scratch_shapes=[pltpu.VMEM((tm + 2 * CONV_HALO, CONV_WIDTH), F32),
                        pltpu.VMEM((tm, HGRN_WIDTH + CONV_WIDTH), BF16)],
        compiler_params=pltpu.CompilerParams(
            dimension_semantics=("parallel", "parallel"), vmem_limit_bytes=VMEM_LIMIT),
        name="out_mlp",
    )(x, oa, ofw, obw, sg, hn_g, u, u, u, cw, cb, lg, lb, wo, nm, w1, w2)


def _rope_tables(L):
    rows = L // GRID_W
    row_idx = jnp.repeat(jnp.arange(rows), GRID_W)
    col_idx = jnp.tile(jnp.arange(GRID_W), rows)
    inv_freq = ROPE_THETA ** (-jnp.arange(0, ROPE_AXIS_DIM, 2, dtype=F32) / ROPE_AXIS_DIM)
    ar = row_idx.astype(F32)[:, None] * inv_freq[None, :]
    ac = col_idx.astype(F32)[:, None] * inv_freq[None, :]
    cos = jnp.concatenate([jnp.cos(ar), jnp.cos(ar), jnp.cos(ac), jnp.cos(ac)], axis=-1)
    sin = jnp.concatenate([-jnp.sin(ar), jnp.sin(ar), -jnp.sin(ac), jnp.sin(ac)], axis=-1)
    reps = LANES // HEAD_DIM
    return jnp.tile(cos, (1, reps)), jnp.tile(sin, (1, reps))


def _tile(L, want):
    t = min(want, L)
    assert L % t == 0, (L, t)
    return t


def kernel(x, w_in, w_out, norm_mix, norm_mlp, q_norm, k_norm, hgrn_lb_fwd, hgrn_lb_bwd, hgrn_norm,
           conv_w, conv_b, conv_ln_g, conv_ln_b, w_mlp_in, w_mlp_out):
    B, L, D = x.shape
    depth = w_in.shape[0]
    assert w_in.shape[2] == IN_COLS and L % GRID_W == 0
    cos, sin = _rope_tables(L)
    row = lambda a: a.reshape(1, -1).astype(F32)
    reps = LANES // HEAD_DIM
    tm_in = _tile(L, 512)
    tq = _tile(L, 512)
    tk = _tile(L // KV_UNROLL, 512)
    t_hg = _tile(L, 256)
    tm_out = _tile(L, 1024)

    for l in range(depth):
        (q, kt, va, hq, hv, lff, kf, lfb, kb, sg, u) = _in_proj(
            x, row(norm_mix[l]), w_in[l].astype(BF16),
            jnp.tile(row(q_norm[l]), (1, reps)), jnp.tile(row(k_norm[l]), (1, reps)),
            cos, sin, hgrn_lb_fwd.astype(F32), hgrn_lb_bwd.astype(F32), layer=l, tm=tm_in)
        o_att = _attention(q, kt, va, tq=tq, tk=tk)
        o_fw = _hgrn(hq, hv, lff, kf, T=t_hg, reverse=False)
        o_bw = _hgrn(hq, hv, lfb, kb, T=t_hg, reverse=True)
        x = _out_mlp(
            x, o_att, o_fw, o_bw, sg, row(hgrn_norm[l]), u,
            conv_w[l].astype(F32), row(conv_b[l]), row(conv_ln_g[l]), row(conv_ln_b[l]),
            w_out[l].astype(BF16), row(norm_mlp[l]), w_mlp_in[l].astype(BF16), w_mlp_out[l].astype(BF16),
            tm=tm_out)
    return x
```

```python
import functools

import jax
import jax.numpy as jnp
from jax import lax
from jax.experimental import pallas as pl
from jax.experimental.pallas import tpu as pltpu

F32 = jnp.float32
BF16 = jnp.bfloat16

EPS = 1e-6
GRID_W = 64
HEAD_DIM = 64
ATT_HEADS = 8
ATT_KV_HEADS = 2
ATT_GROUP = ATT_HEADS // ATT_KV_HEADS
ATT_WIDTH = ATT_HEADS * HEAD_DIM
KV_WIDTH = ATT_KV_HEADS * HEAD_DIM
ROPE_THETA = 10000.0
ROPE_AXIS_DIM = HEAD_DIM // 2
HGRN_WIDTH = 256
HGRN_HEADS = 4
HGRN_CHUNK = 16
CONV_WIDTH = 256
CONV_KERNEL = 31
CONV_PAD = (CONV_KERNEL - 1) // 2
CONV_HALO = 16
BF16_ROWS = 16
V_ROWS = HEAD_DIM + BF16_ROWS
KV_UNROLL = 8

LANES = 128
SUBLANES = 8
LOG2E = 1.4426950408889634
VMEM_LIMIT = 56 * 1024 * 1024

_C_Q = 0
_C_K = _C_Q + ATT_WIDTH
_C_V = _C_K + KV_WIDTH
_C_HQ = _C_V + KV_WIDTH
_C_HI = _C_HQ + HGRN_WIDTH
_C_FF = _C_HI + HGRN_WIDTH
_C_FB = _C_FF + HGRN_WIDTH
_C_HG = _C_FB + HGRN_WIDTH
_C_CA = _C_HG + HGRN_WIDTH
_C_CG = _C_CA + CONV_WIDTH
IN_COLS = _C_CG + CONV_WIDTH


def _dot(a, b):
    return jnp.dot(a, b, preferred_element_type=F32)


def _split_dot(x, m, parts):
    acc = None
    r = x
    for _ in range(parts):
        t = r.astype(BF16)
        acc = _dot(t, m) if acc is None else acc + _dot(t, m)
        r = r - t.astype(F32)
    return acc


def _group_matrix(width, group, value):
    r = lax.broadcasted_iota(jnp.int32, (width, width), 0) // group
    c = lax.broadcasted_iota(jnp.int32, (width, width), 1) // group
    return jnp.where(r == c, value, 0.0).astype(BF16)


def _group_mean_matrix(width, group):
    return _group_matrix(width, group, 1.0 / group)


def _sigmoid(x):
    return 1.0 / (1.0 + jnp.exp(-x))


def _in_proj_kernel(x_ref, g_ref, w_ref, qn_ref, kn_ref, cos_ref, sin_ref, lbf_ref, lbb_ref,
                    qt_out, k_out, vt_out, hq_out, hv_out, lff_out, kf_out, lfb_out, kb_out, sg_out, u_out,
                    *, layer):
    x = x_ref[...]
    ms = jnp.mean(x * x, axis=-1, keepdims=True)
    h = (x * lax.rsqrt(ms + EPS) * g_ref[...]).astype(BF16)

    def proj(lo, width):
        return _dot(h, w_ref[:, lo:lo + width])

    gmat = _group_mean_matrix(LANES, HEAD_DIM)
    cos = cos_ref[...]
    sin = sin_ref[...]
    lane = lax.broadcasted_iota(jnp.int32, (1, LANES), 1)
    first_half = (lane % ROPE_AXIS_DIM) < (ROPE_AXIS_DIM // 2)

    def norm_rope(z, gain, scale):
        zn = z * lax.rsqrt(_split_dot(z * z, gmat, 2) + EPS) * gain
        half = ROPE_AXIS_DIM // 2
        partner = jnp.where(first_half, pltpu.roll(zn, LANES - half, 1), pltpu.roll(zn, half, 1))
        return (zn * cos + partner * sin) * scale

    zq = proj(_C_Q, ATT_WIDTH)
    q_scale = HEAD_DIM ** -0.5 * LOG2E
    for c in range(ATT_WIDTH // LANES):
        qc = norm_rope(zq[:, c * LANES:(c + 1) * LANES], qn_ref[...], q_scale)
        qt_out[c * LANES:(c + 1) * LANES, :] = jnp.transpose(qc).astype(BF16)

    k_out[...] = norm_rope(proj(_C_K, KV_WIDTH), kn_ref[...], 1.0).astype(BF16)

    zvt = jnp.transpose(proj(_C_V, KV_WIDTH)).astype(BF16)
    ones = jnp.ones((V_ROWS - HEAD_DIM, zvt.shape[1]), BF16)
    for hh in range(ATT_KV_HEADS):
        vt_out[hh] = jnp.concatenate([zvt[hh * HEAD_DIM:(hh + 1) * HEAD_DIM, :], ones], axis=0)

    def lower_bound(lb_ref):
        p = lb_ref[...]
        p = jnp.exp(p - jnp.max(p, axis=0, keepdims=True))
        p = p / jnp.sum(p, axis=0, keepdims=True)
        lb = jnp.zeros((1, HGRN_WIDTH), F32)
        for i in range(1, layer + 1):
            lb = lb + p[i:i + 1, :]
        return lb

    def forget(z, lb):
        t = jnp.exp(-z)
        sig = 1.0 / (1.0 + t)
        return jnp.log(lb + (1.0 - lb) * sig), (1.0 - lb) * (t * sig)

    hq_out[...] = proj(_C_HQ, HGRN_WIDTH)
    hv_out[...] = proj(_C_HI, HGRN_WIDTH)
    lf, kk = forget(proj(_C_FF, HGRN_WIDTH), lower_bound(lbf_ref))
    lff_out[...] = lf
    kf_out[...] = kk
    lf, kk = forget(proj(_C_FB, HGRN_WIDTH), lower_bound(lbb_ref))
    lfb_out[...] = lf
    kb_out[...] = kk
    zg = proj(_C_HG, HGRN_WIDTH)
    sg_out[...] = zg * _sigmoid(zg)

    u_out[...] = proj(_C_CA, CONV_WIDTH) * _sigmoid(proj(_C_CG, CONV_WIDTH))


def _in_proj(x, g, w, qn, kn, cos, sin, lbf, lbb, *, layer, tm):
    B, L, D = x.shape
    nt = L // tm
    tok = lambda width: pl.BlockSpec((None, tm, width), lambda b, i: (b, i, 0))
    full = lambda a: pl.BlockSpec(a.shape, lambda b, i: (0,) * a.ndim)
    hg = jax.ShapeDtypeStruct((B, L, HGRN_WIDTH), F32)
    out_shape = (
        jax.ShapeDtypeStruct((B, ATT_WIDTH, L), BF16),
        jax.ShapeDtypeStruct((B, L, KV_WIDTH), BF16),
        jax.ShapeDtypeStruct((B, ATT_KV_HEADS, V_ROWS, L), BF16),
        hg, hg, hg, hg, hg, hg, hg,
        jax.ShapeDtypeStruct((B, L, CONV_WIDTH), F32),
    )
    out_specs = (
        pl.BlockSpec((None, ATT_WIDTH, tm), lambda b, i: (b, 0, i)),
        tok(KV_WIDTH),
        pl.BlockSpec((None, ATT_KV_HEADS, V_ROWS, tm), lambda b, i: (b, 0, 0, i)),
        tok(HGRN_WIDTH), tok(HGRN_WIDTH), tok(HGRN_WIDTH), tok(HGRN_WIDTH), tok(HGRN_WIDTH), tok(HGRN_WIDTH),
        tok(HGRN_WIDTH), tok(CONV_WIDTH),
    )
    in_specs = [
        tok(D), full(g), full(w), full(qn), full(kn),
        pl.BlockSpec((tm, LANES), lambda b, i: (i, 0)),
        pl.BlockSpec((tm, LANES), lambda b, i: (i, 0)),
        full(lbf), full(lbb),
    ]
    return pl.pallas_call(
        functools.partial(_in_proj_kernel, layer=layer),
        grid=(B, nt),
        in_specs=in_specs,
        out_specs=out_specs,
        out_shape=out_shape,
        compiler_params=pltpu.CompilerParams(
            dimension_semantics=("parallel", "parallel"), vmem_limit_bytes=VMEM_LIMIT),
        name=f"in_proj_{layer}",
    )(x, g, w, qn, kn, cos, sin, lbf, lbb)


def _attn_kernel(qt_ref, k_ref, vt_ref, o_ref, qs_ref, s_ref, smax_ref, m_ref, acc_ref, *, tq, tk, nk):
    h = pl.program_id(1)
    qt = qt_ref[...]
    q4 = jnp.concatenate(
        [qt[g * HEAD_DIM:(g + 1) * HEAD_DIM, :] for g in range(ATT_GROUP)], axis=1)
    row_head = lax.broadcasted_iota(jnp.int32, (KV_WIDTH, 1), 0) // HEAD_DIM
    qs_ref[...] = jnp.where(row_head == h, jnp.concatenate([q4] * ATT_KV_HEADS, axis=0), jnp.zeros((), BF16))
    m_ref[...] = jnp.full(m_ref.shape, -jnp.inf, F32)
    acc_ref[...] = jnp.zeros(acc_ref.shape, F32)

    def scores(j, g, slot):
        k0 = pl.multiple_of(j * tk, tk)
        cols = slice(g * tq, (g + 1) * tq)
        s = _dot(k_ref[pl.ds(k0, tk), :], qs_ref[:, cols])
        s_ref[:, cols] = s
        smax_ref[slot, :, cols] = jnp.max(s, axis=0, keepdims=True)

    def step(j, slot, last):
        k0 = pl.multiple_of(j * tk, tk)
        m_prev = m_ref[...]
        m_new = jnp.maximum(m_prev, smax_ref[slot])
        alpha = jnp.exp2(m_prev - m_new)
        vt = vt_ref[:, pl.ds(k0, tk)]
        for g in range(ATT_GROUP):
            cols = slice(g * tq, (g + 1) * tq)
            p = jnp.exp2(s_ref[:, cols] - m_new[:, cols]).astype(BF16)
            acc_ref[:, cols] = alpha[:, cols] * acc_ref[:, cols] + _dot(vt, p)
            if not last:
                scores(j + 1, g, 1 - slot)
        m_ref[...] = m_new

    for g in range(ATT_GROUP):
        scores(0, g, 0)

    def body(jj, carry):
        for u in range(KV_UNROLL):
            step(KV_UNROLL * jj + u, u % 2, False)
        return carry

    lax.fori_loop(0, nk // KV_UNROLL - 1, body, 0)
    for u in range(KV_UNROLL):
        step(nk - KV_UNROLL + u, u % 2, u == KV_UNROLL - 1)
    res = acc_ref[0:HEAD_DIM, :] / acc_ref[HEAD_DIM:HEAD_DIM + 1, :]
    o_ref[...] = jnp.concatenate(
        [jnp.transpose(res[:, g * tq:(g + 1) * tq]) for g in range(ATT_GROUP)], axis=1).astype(o_ref.dtype)


def _attention(qt, k, vat, *, tq, tk):
    B, _, L = qt.shape
    gw = ATT_GROUP * HEAD_DIM
    mq = ATT_GROUP * tq
    assert (L // tk) % KV_UNROLL == 0, "the kv loop handles KV_UNROLL chunks per trip"
    return pl.pallas_call(
        functools.partial(_attn_kernel, tq=tq, tk=tk, nk=L // tk),
        grid=(B, ATT_KV_HEADS, L // tq),
        in_specs=[
            pl.BlockSpec((None, gw, tq), lambda b, h, i: (b, h, i)),
            pl.BlockSpec((None, L, KV_WIDTH), lambda b, h, i: (b, 0, 0)),
            pl.BlockSpec((None, None, V_ROWS, L), lambda b, h, i: (b, h, 0, 0)),
        ],
        out_specs=pl.BlockSpec((None, tq, gw), lambda b, h, i: (b, i, h)),
        out_shape=jax.ShapeDtypeStruct((B, L, ATT_WIDTH), BF16),
        scratch_shapes=[
            pltpu.VMEM((KV_WIDTH, mq), BF16),
            pltpu.VMEM((tk, mq), F32),
            pltpu.VMEM((2, 1, mq), F32),
            pltpu.VMEM((1, mq), F32),
            pltpu.VMEM((V_ROWS, mq), F32),
        ],
        compiler_params=pltpu.CompilerParams(
            dimension_semantics=("parallel", "parallel", "parallel"), vmem_limit_bytes=VMEM_LIMIT),
        name="attention",
    )(qt, k, vat)


def _hgrn_kernel(q_ref, v_ref, lf_ref, k_ref, o_ref, s_ref, *, T, reverse):
    C = HGRN_CHUNK
    W = HGRN_WIDTH

    @pl.when(pl.program_id(1) == 0)
    def _():
        s_ref[...] = jnp.zeros(s_ref.shape, F32)

    q = q_ref[...]
    v = v_ref[...]
    k = k_ref[...]
    lf = lf_ref[...]

    r = lax.broadcasted_iota(jnp.int32, (T, T), 0)
    c = lax.broadcasted_iota(jnp.int32, (T, T), 1)
    same = (r // C) == (c // C)
    tri = same & ((c >= r) if reverse else (c <= r))
    m_tri = jnp.where(tri, 1.0, 0.0).astype(BF16)
    m_all = jnp.where(same, 1.0, 0.0).astype(BF16)

    def left_split_dot(m, x):
        acc = None
        rem = x
        for _ in range(3):
            t = rem.astype(BF16)
            acc = _dot(m, t) if acc is None else acc + _dot(m, t)
            rem = rem - t.astype(F32)
        return acc

    b = left_split_dot(m_tri, lf * LOG2E)
    bl = left_split_dot(m_all, lf * LOG2E)

    zpad = jnp.zeros((SUBLANES, W), F32)
    padded = [jnp.concatenate([a, zpad] if reverse else [zpad, a], axis=0) for a in (k, b, v)]
    pos = lax.broadcasted_iota(jnp.int32, (T, 1), 0) % C
    head_sum = _group_matrix(W, W // HGRN_HEADS, 1.0)
    acc = jnp.zeros((T, W), F32)
    for r in range(SUBLANES):
        if r == 0:
            shifted = padded
        else:
            shifted = [pltpu.roll(a, (T + SUBLANES - r) if reverse else r, 0) for a in padded]
        for d in range(r, C, SUBLANES):
            near = (d == r) != reverse
            lo = SUBLANES if near else 0
            ks, bs, vs = (a[lo:lo + T, :] for a in shifted)
            valid = (pos <= C - 1 - d) if reverse else (pos >= d)
            w = jnp.where(valid, q * ks * jnp.exp2(b - bs), 0.0)
            acc = acc + _dot(w.astype(BF16), head_sum) * vs

    qt = (q * jnp.exp2(b)).astype(BF16)
    kt = (k * jnp.exp2(bl - b)).astype(BF16)
    dec = jnp.exp2(bl)
    vb = v.astype(BF16)
    dh = W // HGRN_HEADS
    lane = lax.broadcasted_iota(jnp.int32, (1, W), 1)
    hmask = [(lane // dh) == hh for hh in range(HGRN_HEADS)]
    zero = jnp.zeros((), BF16)

    s = s_ref[...]
    order = range(T // C - 1, -1, -1) if reverse else range(T // C)
    for n in order:
        r0 = n * C
        qn = qt[r0:r0 + C]
        kn = kt[r0:r0 + C]
        vn = vb[r0:r0 + C]
        qexp = jnp.concatenate([jnp.where(hmask[hh], qn, zero) for hh in range(HGRN_HEADS)], axis=0)
        o4 = lax.dot_general(qexp, s.astype(BF16), (((1,), (1,)), ((), ())), preferred_element_type=F32)
        on = jnp.concatenate([o4[hh * C:(hh + 1) * C, :] for hh in range(HGRN_HEADS)], axis=1)
        o_ref[r0:r0 + C, :] = acc[r0:r0 + C] + on
        kexp = jnp.concatenate([jnp.where(hmask[hh], kn, zero) for hh in range(HGRN_HEADS)], axis=0)
        vexp = jnp.concatenate([vn[:, hh * dh:(hh + 1) * dh] for hh in range(HGRN_HEADS)], axis=0)
        u = lax.dot_general(vexp, kexp, (((0,), (0,)), ((), ())), preferred_element_type=F32)
        s = s * dec[r0:r0 + 1, :] + u
    s_ref[...] = s


def _hgrn(q, v, lf, k, *, T, reverse):
    B, L, W = q.shape
    nt = L // T
    idx = (lambda b, i: (b, nt - 1 - i, 0)) if reverse else (lambda b, i: (b, i, 0))
    spec = pl.BlockSpec((None, T, W), idx)
    return pl.pallas_call(
        functools.partial(_hgrn_kernel, T=T, reverse=reverse),
        grid=(B, nt),
        in_specs=[spec, spec, spec, spec],
        out_specs=spec,
        out_shape=jax.ShapeDtypeStruct((B, L, W), F32),
        scratch_shapes=[pltpu.VMEM((W // HGRN_HEADS, W), F32)],
        compiler_params=pltpu.CompilerParams(
            dimension_semantics=("parallel", "arbitrary"), vmem_limit_bytes=VMEM_LIMIT),
        name="hgrn_bwd" if reverse else "hgrn_fwd",
    )(q, v, lf, k)


def _out_mlp_kernel(x_ref, oa_ref, ofw_ref, obw_ref, sg_ref, hn_ref, up_ref, uc_ref, un_ref,
                    cw_ref, cb_ref, lg_ref, lb_ref, wo_ref, nm_ref, w1_ref, w2_ref, out_ref, ext_ref, stage_ref,
                    *, tm, sub, ff_chunk):
    i = pl.program_id(1)
    nt = pl.num_programs(1)

    ext_ref[0:CONV_HALO, :] = jnp.where(i > 0, up_ref[...], 0.0)
    ext_ref[CONV_HALO:CONV_HALO + tm, :] = uc_ref[...]
    ext_ref[CONV_HALO + tm:, :] = jnp.where(i < nt - 1, un_ref[...], 0.0)
    gmat = _group_mean_matrix(HGRN_WIDTH, HGRN_WIDTH // HGRN_HEADS)

    def mixer_inputs(r0):
        rows = slice(r0, r0 + sub)
        o = ofw_ref[rows, :] + obw_ref[rows, :]
        o_hg = o * lax.rsqrt(_split_dot(o * o, gmat, 2) + EPS) * hn_ref[...] * sg_ref[rows, :]

        first = CONV_HALO - CONV_PAD
        y = None
        for r in range(SUBLANES):
            z = None
            for w in range(CONV_KERNEL):
                if (first + w) % SUBLANES == r:
                    term = ext_ref[pl.ds(r0 + first + w - r, sub + SUBLANES), :] * cw_ref[w:w + 1, :]
                    z = term if z is None else z + term
            if r:
                z = pltpu.roll(z, sub + SUBLANES - r, 0)
            y = z[:sub, :] if y is None else y + z[:sub, :]
        y = y + cb_ref[...]
        mu = jnp.mean(y, axis=-1, keepdims=True)
        yc = y - mu
        yn = yc * lax.rsqrt(jnp.mean(yc * yc, axis=-1, keepdims=True) + EPS) * lg_ref[...] + lb_ref[...]
        o_cv = yn * _sigmoid(yn)
        stage_ref[rows, 0:HGRN_WIDTH] = o_hg.astype(BF16)
        stage_ref[rows, HGRN_WIDTH:] = o_cv.astype(BF16)

    def project_mlp(r0):
        rows = slice(r0, r0 + sub)
        mix = (_dot(oa_ref[rows, :], wo_ref[0:ATT_WIDTH, :])
               + _dot(stage_ref[rows, :], wo_ref[ATT_WIDTH:, :]))
        x1 = x_ref[rows, :] + mix

        ms = jnp.mean(x1 * x1, axis=-1, keepdims=True)
        hn = (x1 * lax.rsqrt(ms + EPS) * nm_ref[...]).astype(BF16)
        mlp = None
        d_ff = w1_ref.shape[1]
        for c0 in range(0, d_ff, ff_chunk):
            a = jnp.maximum(_dot(hn, w1_ref[:, c0:c0 + ff_chunk]), 0.0)
            part = _dot((a * a).astype(BF16), w2_ref[c0:c0 + ff_chunk, :])
            mlp = part if mlp is None else mlp + part
        out_ref[rows, :] = x1 + mlp

    for r0 in range(0, tm, sub):
        mixer_inputs(r0)
    for r0 in range(0, tm, sub):
        project_mlp(r0)


def _out_mlp(x, oa, ofw, obw, sg, hn_g, u, cw, cb, lg, lb, wo, nm, w1, w2, *, tm):
    B, L, D = x.shape
    nt = L // tm
    hb = tm // CONV_HALO
    nhb = L // CONV_HALO
    tok = lambda width: pl.BlockSpec((None, tm, width), lambda b, i: (b, i, 0))
    const = lambda a: pl.BlockSpec(a.shape, lambda b, i: (0,) * a.ndim, pipeline_mode=pl.Buffered(1))
    in_specs = [
        tok(D), tok(ATT_WIDTH), tok(HGRN_WIDTH), tok(HGRN_WIDTH), tok(HGRN_WIDTH), const(hn_g),
        pl.BlockSpec((None, CONV_HALO, CONV_WIDTH), lambda b, i: (b, jnp.maximum(i * hb - 1, 0), 0)),
        tok(CONV_WIDTH),
        pl.BlockSpec((None, CONV_HALO, CONV_WIDTH), lambda b, i: (b, jnp.minimum((i + 1) * hb, nhb - 1), 0)),
        const(cw), const(cb), const(lg), const(lb), const(wo), const(nm), const(w1), const(w2),
    ]
    return pl.pallas_call(
        functools.partial(_out_mlp_kernel, tm=tm, sub=min(512, tm), ff_chunk=min(1024, w1.shape[1])),
        grid=(B, nt),
        in_specs=in_specs,
        out_specs=tok(D),
        out_shape=jax.ShapeDtypeStruct((B, L, D), F32),
        scratch_shapes=[pltpu.VMEM((tm + 2 * CONV_HALO, CONV_WIDTH), F32),
                        pltpu.VMEM((tm, HGRN_WIDTH + CONV_WIDTH), BF16)],
        compiler_params=pltpu.CompilerParams(
            dimension_semantics=("parallel", "parallel"), vmem_limit_bytes=VMEM_LIMIT),
        name="out_mlp",
    )(x, oa, ofw, obw, sg, hn_g, u, u, u, cw, cb, lg, lb, wo, nm, w1, w2)


def _rope_tables(L):
    rows = L // GRID_W
    row_idx = jnp.repeat(jnp.arange(rows), GRID_W)
    col_idx = jnp.tile(jnp.arange(GRID_W), rows)
    inv_freq = ROPE_THETA ** (-jnp.arange(0, ROPE_AXIS_DIM, 2, dtype=F32) / ROPE_AXIS_DIM)
    ar = row_idx.astype(F32)[:, None] * inv_freq[None, :]
    ac = col_idx.astype(F32)[:, None] * inv_freq[None, :]
    cos = jnp.concatenate([jnp.cos(ar), jnp.cos(ar), jnp.cos(ac), jnp.cos(ac)], axis=-1)
    sin = jnp.concatenate([-jnp.sin(ar), jnp.sin(ar), -jnp.sin(ac), jnp.sin(ac)], axis=-1)
    reps = LANES // HEAD_DIM
    return jnp.tile(cos, (1, reps)), jnp.tile(sin, (1, reps))


def _tile(L, want):
    t = min(want, L)
    assert L % t == 0, (L, t)
    return t


def kernel(x, w_in, w_out, norm_mix, norm_mlp, q_norm, k_norm, hgrn_lb_fwd, hgrn_lb_bwd, hgrn_norm,
           conv_w, conv_b, conv_ln_g, conv_ln_b, w_mlp_in, w_mlp_out):
    B, L, D = x.shape
    depth = w_in.shape[0]
    assert w_in.shape[2] == IN_COLS and L % GRID_W == 0
    cos, sin = _rope_tables(L)
    row = lambda a: a.reshape(1, -1).astype(F32)
    reps = LANES // HEAD_DIM
    tm_in = _tile(L, 512)
    tq = _tile(L, 512)
    tk = _tile(L // KV_UNROLL, 512)
    t_hg = _tile(L, 256)
    tm_out = _tile(L, 1024)

    for l in range(depth):
        (q, kt, va, hq, hv, lff, kf, lfb, kb, sg, u) = _in_proj(
            x, row(norm_mix[l]), w_in[l].astype(BF16),
            jnp.tile(row(q_norm[l]), (1, reps)), jnp.tile(row(k_norm[l]), (1, reps)),
            cos, sin, hgrn_lb_fwd.astype(F32), hgrn_lb_bwd.astype(F32), layer=l, tm=tm_in)
        o_att = _attention(q, kt, va, tq=tq, tk=tk)
        o_fw = _hgrn(hq, hv, lff, kf, T=t_hg, reverse=False)
        o_bw = _hgrn(hq, hv, lfb, kb, T=t_hg, reverse=True)
        x = _out_mlp(
            x, o_att, o_fw, o_bw, sg, row(hgrn_norm[l]), u,
            conv_w[l].astype(F32), row(conv_b[l]), row(conv_ln_g[l]), row(conv_ln_b[l]),
            w_out[l].astype(BF16), row(norm_mlp[l]), w_mlp_in[l].astype(BF16), w_mlp_out[l].astype(BF16),
            tm=tm_out)
    return x
```

```python
import functools

import jax
import jax.numpy as jnp
from jax import lax
from jax.experimental import pallas as pl
from jax.experimental.pallas import tpu as pltpu

F32 = jnp.float32
BF16 = jnp.bfloat16

EPS = 1e-6
GRID_W = 64
HEAD_DIM = 64
ATT_HEADS = 8
ATT_KV_HEADS = 2
ATT_GROUP = ATT_HEADS // ATT_KV_HEADS
ATT_WIDTH = ATT_HEADS * HEAD_DIM
KV_WIDTH = ATT_KV_HEADS * HEAD_DIM
ROPE_THETA = 10000.0
ROPE_AXIS_DIM = HEAD_DIM // 2
HGRN_WIDTH = 256
HGRN_HEADS = 4
HGRN_CHUNK = 16
CONV_WIDTH = 256
CONV_KERNEL = 31
CONV_PAD = (CONV_KERNEL - 1) // 2
CONV_HALO = 16
V_ROWS = 2 * HEAD_DIM
KV_UNROLL = 8

LANES = 128
SUBLANES = 8
LOG2E = 1.4426950408889634
VMEM_LIMIT = 56 * 1024 * 1024

_C_Q = 0
_C_K = _C_Q + ATT_WIDTH
_C_V = _C_K + KV_WIDTH
_C_HQ = _C_V + KV_WIDTH
_C_HI = _C_HQ + HGRN_WIDTH
_C_FF = _C_HI + HGRN_WIDTH
_C_FB = _C_FF + HGRN_WIDTH
_C_HG = _C_FB + HGRN_WIDTH
_C_CA = _C_HG + HGRN_WIDTH
_C_CG = _C_CA + CONV_WIDTH
IN_COLS = _C_CG + CONV_WIDTH


def _dot(a, b):
    return jnp.dot(a, b, preferred_element_type=F32)


def _split_dot(x, m, parts):
    acc = None
    r = x
    for _ in range(parts):
        t = r.astype(BF16)
        acc = _dot(t, m) if acc is None else acc + _dot(t, m)
        r = r - t.astype(F32)
    return acc


def _group_matrix(width, group, value):
    r = lax.broadcasted_iota(jnp.int32, (width, width), 0) // group
    c = lax.broadcasted_iota(jnp.int32, (width, width), 1) // group
    return jnp.where(r == c, value, 0.0).astype(BF16)


def _group_mean_matrix(width, group):
    return _group_matrix(width, group, 1.0 / group)


def _sigmoid(x):
    return 1.0 / (1.0 + jnp.exp(-x))


def _in_proj_kernel(x_ref, g_ref, w_ref, qn_ref, kn_ref, cos_ref, sin_ref, lbf_ref, lbb_ref,
                    qt_out, k_out, vt_out, hq_out, hv_out, lff_out, kf_out, lfb_out, kb_out, sg_out, u_out,
                    *, layer):
    x = x_ref[...]
    ms = jnp.mean(x * x, axis=-1, keepdims=True)
    h = (x * lax.rsqrt(ms + EPS) * g_ref[...]).astype(BF16)

    def proj(lo, width):
        return _dot(h, w_ref[:, lo:lo + width])

    gmat = _group_mean_matrix(LANES, HEAD_DIM)
    cos = cos_ref[...]
    sin = sin_ref[...]
    lane = lax.broadcasted_iota(jnp.int32, (1, LANES), 1)
    first_half = (lane % ROPE_AXIS_DIM) < (ROPE_AXIS_DIM // 2)

    def norm_rope(z, gain, scale):
        zn = z * lax.rsqrt(_split_dot(z * z, gmat, 2) + EPS) * gain
        half = ROPE_AXIS_DIM // 2
        partner = jnp.where(first_half, pltpu.roll(zn, LANES - half, 1), pltpu.roll(zn, half, 1))
        return (zn * cos + partner * sin) * scale

    zq = proj(_C_Q, ATT_WIDTH)
    q_scale = HEAD_DIM ** -0.5 * LOG2E
    for c in range(ATT_WIDTH // LANES):
        qc = norm_rope(zq[:, c * LANES:(c + 1) * LANES], qn_ref[...], q_scale)
        qt_out[c * LANES:(c + 1) * LANES, :] = jnp.transpose(qc).astype(BF16)

    k_out[...] = norm_rope(proj(_C_K, KV_WIDTH), kn_ref[...], 1.0).astype(BF16)

    zvt = jnp.transpose(proj(_C_V, KV_WIDTH)).astype(BF16)
    ones = jnp.ones((V_ROWS - HEAD_DIM, zvt.shape[1]), BF16)
    for hh in range(ATT_KV_HEADS):
        vt_out[hh] = jnp.concatenate([zvt[hh * HEAD_DIM:(hh + 1) * HEAD_DIM, :], ones], axis=0)

    def lower_bound(lb_ref):
        p = lb_ref[...]
        p = jnp.exp(p - jnp.max(p, axis=0, keepdims=True))
        p = p / jnp.sum(p, axis=0, keepdims=True)
        lb = jnp.zeros((1, HGRN_WIDTH), F32)
        for i in range(1, layer + 1):
            lb = lb + p[i:i + 1, :]
        return lb

    def forget(z, lb):
        t = jnp.exp(-z)
        sig = 1.0 / (1.0 + t)
        return jnp.log(lb + (1.0 - lb) * sig), (1.0 - lb) * (t * sig)

    hq_out[...] = proj(_C_HQ, HGRN_WIDTH)
    hv_out[...] = proj(_C_HI, HGRN_WIDTH)
    lf, kk = forget(proj(_C_FF, HGRN_WIDTH), lower_bound(lbf_ref))
    lff_out[...] = lf
    kf_out[...] = kk
    lf, kk = forget(proj(_C_FB, HGRN_WIDTH), lower_bound(lbb_ref))
    lfb_out[...] = lf
    kb_out[...] = kk
    zg = proj(_C_HG, HGRN_WIDTH)
    sg_out[...] = zg * _sigmoid(zg)

    u_out[...] = proj(_C_CA, CONV_WIDTH) * _sigmoid(proj(_C_CG, CONV_WIDTH))


def _in_proj(x, g, w, qn, kn, cos, sin, lbf, lbb, *, layer, tm):
    B, L, D = x.shape
    nt = L // tm
    tok = lambda width: pl.BlockSpec((None, tm, width), lambda b, i: (b, i, 0))
    full = lambda a: pl.BlockSpec(a.shape, lambda b, i: (0,) * a.ndim)
    hg = jax.ShapeDtypeStruct((B, L, HGRN_WIDTH), F32)
    out_shape = (
        jax.ShapeDtypeStruct((B, ATT_WIDTH, L), BF16),
        jax.ShapeDtypeStruct((B, L, KV_WIDTH), BF16),
        jax.ShapeDtypeStruct((B, ATT_KV_HEADS, V_ROWS, L), BF16),
        hg, hg, hg, hg, hg, hg, hg,
        jax.ShapeDtypeStruct((B, L, CONV_WIDTH), F32),
    )
    out_specs = (
        pl.BlockSpec((None, ATT_WIDTH, tm), lambda b, i: (b, 0, i)),
        tok(KV_WIDTH),
        pl.BlockSpec((None, ATT_KV_HEADS, V_ROWS, tm), lambda b, i: (b, 0, 0, i)),
        tok(HGRN_WIDTH), tok(HGRN_WIDTH), tok(HGRN_WIDTH), tok(HGRN_WIDTH), tok(HGRN_WIDTH), tok(HGRN_WIDTH),
        tok(HGRN_WIDTH), tok(CONV_WIDTH),
    )
    in_specs = [
        tok(D), full(g), full(w), full(qn), full(kn),
        pl.BlockSpec((tm, LANES), lambda b, i: (i, 0)),
        pl.BlockSpec((tm, LANES), lambda b, i: (i, 0)),
        full(lbf), full(lbb),
    ]
    return pl.pallas_call(
        functools.partial(_in_proj_kernel, layer=layer),
        grid=(B, nt),
        in_specs=in_specs,
        out_specs=out_specs,
        out_shape=out_shape,
        compiler_params=pltpu.CompilerParams(
            dimension_semantics=("parallel", "parallel"), vmem_limit_bytes=VMEM_LIMIT),
        name=f"in_proj_{layer}",
    )(x, g, w, qn, kn, cos, sin, lbf, lbb)


def _attn_kernel(qt_ref, k_ref, vt_ref, o_ref, qs_ref, s_ref, smax_ref, m_ref, acc_ref, *, tq, tk, nk):
    h = pl.program_id(1)
    qt = qt_ref[...]
    q4 = jnp.concatenate(
        [qt[g * HEAD_DIM:(g + 1) * HEAD_DIM, :] for g in range(ATT_GROUP)], axis=1)
    row_head = lax.broadcasted_iota(jnp.int32, (KV_WIDTH, 1), 0) // HEAD_DIM
    qs_ref[...] = jnp.where(row_head == h, jnp.concatenate([q4] * ATT_KV_HEADS, axis=0), jnp.zeros((), BF16))
    m_ref[...] = jnp.full(m_ref.shape, -jnp.inf, F32)
    acc_ref[...] = jnp.zeros(acc_ref.shape, F32)

    def scores(j, g, slot):
        k0 = pl.multiple_of(j * tk, tk)
        cols = slice(g * tq, (g + 1) * tq)
        s = _dot(k_ref[pl.ds(k0, tk), :], qs_ref[:, cols])
        s_ref[:, cols] = s
        smax_ref[slot, :, cols] = jnp.max(s, axis=0, keepdims=True)

    def step(j, slot, last):
        k0 = pl.multiple_of(j * tk, tk)
        m_prev = m_ref[...]
        m_new = jnp.maximum(m_prev, smax_ref[slot])
        alpha = jnp.exp2(m_prev - m_new)
        vt = vt_ref[:, pl.ds(k0, tk)]
        for g in range(ATT_GROUP):
            cols = slice(g * tq, (g + 1) * tq)
            p = jnp.exp2(s_ref[:, cols] - m_new[:, cols]).astype(BF16)
            acc_ref[:, cols] = alpha[:, cols] * acc_ref[:, cols] + _dot(vt, p)
            if not last:
                scores(j + 1, g, 1 - slot)
        m_ref[...] = m_new

    for g in range(ATT_GROUP):
        scores(0, g, 0)

    def body(jj, carry):
        for u in range(KV_UNROLL):
            step(KV_UNROLL * jj + u, u % 2, False)
        return carry

    lax.fori_loop(0, nk // KV_UNROLL - 1, body, 0)
    for u in range(KV_UNROLL):
        step(nk - KV_UNROLL + u, u % 2, u == KV_UNROLL - 1)
    res = acc_ref[0:HEAD_DIM, :] / acc_ref[HEAD_DIM:HEAD_DIM + 1, :]
    o_ref[...] = jnp.concatenate(
        [jnp.transpose(res[:, g * tq:(g + 1) * tq]) for g in range(ATT_GROUP)], axis=1).astype(o_ref.dtype)


def _attention(qt, k, vat, *, tq, tk):
    B, _, L = qt.shape
    gw = ATT_GROUP * HEAD_DIM
    mq = ATT_GROUP * tq
    assert (L // tk) % KV_UNROLL == 0, "the kv loop handles KV_UNROLL chunks per trip"
    return pl.pallas_call(
        functools.partial(_attn_kernel, tq=tq, tk=tk, nk=L // tk),
        grid=(B, ATT_KV_HEADS, L // tq),
        in_specs=[
            pl.BlockSpec((None, gw, tq), lambda b, h, i: (b, h, i)),
            pl.BlockSpec((None, L, KV_WIDTH), lambda b, h, i: (b, 0, 0)),
            pl.BlockSpec((None, None, V_ROWS, L), lambda b, h, i: (b, h, 0, 0)),
        ],
        out_specs=pl.BlockSpec((None, tq, gw), lambda b, h, i: (b, i, h)),
        out_shape=jax.ShapeDtypeStruct((B, L, ATT_WIDTH), BF16),
        scratch_shapes=[
            pltpu.VMEM((KV_WIDTH, mq), BF16),
            pltpu.VMEM((tk, mq), F32),
            pltpu.VMEM((2, 1, mq), F32),
            pltpu.VMEM((1, mq), F32),
            pltpu.VMEM((V_ROWS, mq), F32),
        ],
        compiler_params=pltpu.CompilerParams(
            dimension_semantics=("parallel", "parallel", "parallel"), vmem_limit_bytes=VMEM_LIMIT),
        name="attention",
    )(qt, k, vat)


def _hgrn_kernel(q_ref, v_ref, lf_ref, k_ref, o_ref, s_ref, *, T, reverse):
    C = HGRN_CHUNK
    W = HGRN_WIDTH

    @pl.when(pl.program_id(1) == 0)
    def _():
        s_ref[...] = jnp.zeros(s_ref.shape, F32)

    q = q_ref[...]
    v = v_ref[...]
    k = k_ref[...]
    lf = lf_ref[...]

    r = lax.broadcasted_iota(jnp.int32, (T, T), 0)
    c = lax.broadcasted_iota(jnp.int32, (T, T), 1)
    same = (r // C) == (c // C)
    tri = same & ((c >= r) if reverse else (c <= r))
    m_tri = jnp.where(tri, 1.0, 0.0).astype(BF16)
    m_all = jnp.where(same, 1.0, 0.0).astype(BF16)

    def left_split_dot(m, x):
        acc = None
        rem = x
        for _ in range(3):
            t = rem.astype(BF16)
            acc = _dot(m, t) if acc is None else acc + _dot(m, t)
            rem = rem - t.astype(F32)
        return acc

    b = left_split_dot(m_tri, lf * LOG2E)
    bl = left_split_dot(m_all, lf * LOG2E)

    zpad = jnp.zeros((SUBLANES, W), F32)
    padded = [jnp.concatenate([a, zpad] if reverse else [zpad, a], axis=0) for a in (k, b, v)]
    pos = lax.broadcasted_iota(jnp.int32, (T, 1), 0) % C
    head_sum = _group_matrix(W, W // HGRN_HEADS, 1.0)
    acc = jnp.zeros((T, W), F32)
    for r in range(SUBLANES):
        if r == 0:
            shifted = padded
        else:
            shifted = [pltpu.roll(a, (T + SUBLANES - r) if reverse else r, 0) for a in padded]
        for d in range(r, C, SUBLANES):
            near = (d == r) != reverse
            lo = SUBLANES if near else 0
            ks, bs, vs = (a[lo:lo + T, :] for a in shifted)
            valid = (pos <= C - 1 - d) if reverse else (pos >= d)
            w = jnp.where(valid, q * ks * jnp.exp2(b - bs), 0.0)
            acc = acc + _dot(w.astype(BF16), head_sum) * vs

    qt = (q * jnp.exp2(b)).astype(BF16)
    kt = (k * jnp.exp2(bl - b)).astype(BF16)
    dec = jnp.exp2(bl)
    vb = v.astype(BF16)
    dh = W // HGRN_HEADS
    lane = lax.broadcasted_iota(jnp.int32, (1, W), 1)
    hmask = [(lane // dh) == hh for hh in range(HGRN_HEADS)]
    zero = jnp.zeros((), BF16)

    s = s_ref[...]
    order = range(T // C - 1, -1, -1) if reverse else range(T // C)
    for n in order:
        r0 = n * C
        qn = qt[r0:r0 + C]
        kn = kt[r0:r0 + C]
        vn = vb[r0:r0 + C]
        qexp = jnp.concatenate([jnp.where(hmask[hh], qn, zero) for hh in range(HGRN_HEADS)], axis=0)
        o4 = lax.dot_general(qexp, s.astype(BF16), (((1,), (1,)), ((), ())), preferred_element_type=F32)
        on = jnp.concatenate([o4[hh * C:(hh + 1) * C, :] for hh in range(HGRN_HEADS)], axis=1)
        o_ref[r0:r0 + C, :] = acc[r0:r0 + C] + on
        kexp = jnp.concatenate([jnp.where(hmask[hh], kn, zero) for hh in range(HGRN_HEADS)], axis=0)
        vexp = jnp.concatenate([vn[:, hh * dh:(hh + 1) * dh] for hh in range(HGRN_HEADS)], axis=0)
        u = lax.dot_general(vexp, kexp, (((0,), (0,)), ((), ())), preferred_element_type=F32)
        s = s * dec[r0:r0 + 1, :] + u
    s_ref[...] = s


def _hgrn(q, v, lf, k, *, T, reverse):
    B, L, W = q.shape
    nt = L // T
    idx = (lambda b, i: (b, nt - 1 - i, 0)) if reverse else (lambda b, i: (b, i, 0))
    spec = pl.BlockSpec((None, T, W), idx)
    return pl.pallas_call(
        functools.partial(_hgrn_kernel, T=T, reverse=reverse),
        grid=(B, nt),
        in_specs=[spec, spec, spec, spec],
        out_specs=spec,
        out_shape=jax.ShapeDtypeStruct((B, L, W), F32),
        scratch_shapes=[pltpu.VMEM((W // HGRN_HEADS, W), F32)],
        compiler_params=pltpu.CompilerParams(
            dimension_semantics=("parallel", "arbitrary"), vmem_limit_bytes=VMEM_LIMIT),
        name="hgrn_bwd" if reverse else "hgrn_fwd",
    )(q, v, lf, k)


def _out_mlp_kernel(x_ref, oa_ref, ofw_ref, obw_ref, sg_ref, hn_ref, up_ref, uc_ref, un_ref,
                    cw_ref, cb_ref, lg_ref, lb_ref, wo_ref, nm_ref, w1_ref, w2_ref, out_ref, ext_ref, stage_ref,
                    *, tm, sub, ff_chunk):
    i = pl.program_id(1)
    nt = pl.num_programs(1)

    ext_ref[0:CONV_HALO, :] = jnp.where(i > 0, up_ref[...], 0.0)
    ext_ref[CONV_HALO:CONV_HALO + tm, :] = uc_ref[...]
    ext_ref[CONV_HALO + tm:, :] = jnp.where(i < nt - 1, un_ref[...], 0.0)
    gmat = _group_mean_matrix(HGRN_WIDTH, HGRN_WIDTH // HGRN_HEADS)

    def mixer_inputs(r0, n):
        rows = slice(r0, r0 + n)
        o = ofw_ref[rows, :] + obw_ref[rows, :]
        o_hg = o * lax.rsqrt(_split_dot(o * o, gmat, 2) + EPS) * hn_ref[...] * sg_ref[rows, :]

        first = CONV_HALO - CONV_PAD
        y = None
        for r in range(SUBLANES):
            z = None
            for w in range(CONV_KERNEL):
                if (first + w) % SUBLANES == r:
                    term = ext_ref[pl.ds(r0 + first + w - r, n + SUBLANES), :] * cw_ref[w:w + 1, :]
                    z = term if z is None else z + term
            if r:
                z = pltpu.roll(z, n + SUBLANES - r, 0)
            y = z[:n, :] if y is None else y + z[:n, :]
        y = y + cb_ref[...]
        mu = jnp.mean(y, axis=-1, keepdims=True)
        yc = y - mu
        yn = yc * lax.rsqrt(jnp.mean(yc * yc, axis=-1, keepdims=True) + EPS) * lg_ref[...] + lb_ref[...]
        o_cv = yn * _sigmoid(yn)
        stage_ref[rows, 0:HGRN_WIDTH] = o_hg.astype(BF16)
        stage_ref[rows, HGRN_WIDTH:] = o_cv.astype(BF16)

    def project_mlp(r0):
        rows = slice(r0, r0 + sub)
        mix = (_dot(oa_ref[rows, :], wo_ref[0:ATT_WIDTH, :])
               + _dot(stage_ref[rows, :], wo_ref[ATT_WIDTH:, :]))
        x1 = x_ref[rows, :] + mix

        ms = jnp.mean(x1 * x1, axis=-1, keepdims=True)
        hn = (x1 * lax.rsqrt(ms + EPS) * nm_ref[...]).astype(BF16)
        mlp = None
        d_ff = w1_ref.shape[1]
        for c0 in range(0, d_ff, ff_chunk):
            a = jnp.maximum(_dot(hn, w1_ref[:, c0:c0 + ff_chunk]), 0.0)
            part = _dot((a * a).astype(BF16), w2_ref[c0:c0 + ff_chunk, :])
            mlp = part if mlp is None else mlp + part
        out_ref[rows, :] = x1 + mlp

    for r0 in range(0, tm, sub):
        mixer_inputs(r0, sub)
    for r0 in range(0, tm, sub):
        project_mlp(r0)


def _out_mlp(x, oa, ofw, obw, sg, hn_g, u, cw, cb, lg, lb, wo, nm, w1, w2, *, tm):
    B, L, D = x.shape
    nt = L // tm
    hb = tm // CONV_HALO
    nhb = L // CONV_HALO
    tok = lambda width: pl.BlockSpec((None, tm, width), lambda b, i: (b, i, 0))
    const = lambda a: pl.BlockSpec(a.shape, lambda b, i: (0,) * a.ndim, pipeline_mode=pl.Buffered(1))
    in_specs = [
        tok(D), tok(ATT_WIDTH), tok(HGRN_WIDTH), tok(HGRN_WIDTH), tok(HGRN_WIDTH), const(hn_g),
        pl.BlockSpec((None, CONV_HALO, CONV_WIDTH), lambda b, i: (b, jnp.maximum(i * hb - 1, 0), 0)),
        tok(CONV_WIDTH),
        pl.BlockSpec((None, CONV_HALO, CONV_WIDTH), lambda b, i: (b, jnp.minimum((i + 1) * hb, nhb - 1), 0)),
        const(cw), const(cb), const(lg), const(lb), const(wo), const(nm), const(w1), const(w2),
    ]
    return pl.pallas_call(
        functools.partial(_out_mlp_kernel, tm=tm, sub=min(512, tm), ff_chunk=min(1024, w1.shape[1])),
        grid=(B, nt),
        in_specs=in_specs,
        out_specs=tok(D),
        out_shape=jax.ShapeDtypeStruct((B, L, D), F32),
        scratch_shapes=[pltpu.VMEM((tm + 2 * CONV_HALO, CONV_WIDTH), F32),
                        pltpu.VMEM((tm, HGRN_WIDTH + CONV_WIDTH), BF16)],
        compiler_params=pltpu.CompilerParams(
            dimension_semantics=("parallel", "parallel"), vmem_limit_bytes=VMEM_LIMIT),
        name="out_mlp",
    )(x, oa, ofw, obw, sg, hn_g, u, u, u, cw, cb, lg, lb, wo, nm, w1, w2)


def _rope_tables(L):
    rows = L // GRID_W
    row_idx = jnp.repeat(jnp.arange(rows), GRID_W)
    col_idx = jnp.tile(jnp.arange(GRID_W), rows)
    inv_freq = ROPE_THETA ** (-jnp.arange(0, ROPE_AXIS_DIM, 2, dtype=F32) / ROPE_AXIS_DIM)
    ar = row_idx.astype(F32)[:, None] * inv_freq[None, :]
    ac = col_idx.astype(F32)[:, None] * inv_freq[None, :]
    cos = jnp.concatenate([jnp.cos(ar), jnp.cos(ar), jnp.cos(ac), jnp.cos(ac)], axis=-1)
    sin = jnp.concatenate([-jnp.sin(ar), jnp.sin(ar), -jnp.sin(ac), jnp.sin(ac)], axis=-1)
    reps = LANES // HEAD_DIM
    return jnp.tile(cos, (1, reps)), jnp.tile(sin, (1, reps))


def _tile(L, want):
    t = min(want, L)
    assert L % t == 0, (L, t)
    return t


def kernel(x, w_in, w_out, norm_mix, norm_mlp, q_norm, k_norm, hgrn_lb_fwd, hgrn_lb_bwd, hgrn_norm,
           conv_w, conv_b, conv_ln_g, conv_ln_b, w_mlp_in, w_mlp_out):
    B, L, D = x.shape
    depth = w_in.shape[0]
    assert w_in.shape[2] == IN_COLS and L % GRID_W == 0
    cos, sin = _rope_tables(L)
    row = lambda a: a.reshape(1, -1).astype(F32)
    reps = LANES // HEAD_DIM
    tm_in = _tile(L, 512)
    tq = _tile(L, 512)
    tk = _tile(L // KV_UNROLL, 512)
    t_hg = _tile(L, 256)
    tm_out = _tile(L, 1024)

    for l in range(depth):
        (q, kt, va, hq, hv, lff, kf, lfb, kb, sg, u) = _in_proj(
            x, row(norm_mix[l]), w_in[l].astype(BF16),
            jnp.tile(row(q_norm[l]), (1, reps)), jnp.tile(row(k_norm[l]), (1, reps)),
            cos, sin, hgrn_lb_fwd.astype(F32), hgrn_lb_bwd.astype(F32), layer=l, tm=tm_in)
        o_att = _attention(q, kt, va, tq=tq, tk=tk)
        o_fw = _hgrn(hq, hv, lff, kf, T=t_hg, reverse=False)
        o_bw = _hgrn(hq, hv, lfb, kb, T=t_hg, reverse=True)
        x = _out_mlp(
            x, o_att, o_fw, o_bw, sg, row(hgrn_norm[l]), u,
            conv_w[l].astype(F32), row(conv_b[l]), row(conv_ln_g[l]), row(conv_ln_b[l]),
            w_out[l].astype(BF16), row(norm_mlp[l]), w_mlp_in[l].astype(BF16), w_mlp_out[l].astype(BF16),
            tm=tm_out)
    return x
```

```python
import functools

import jax
import jax.numpy as jnp
from jax import lax
from jax.experimental import pallas as pl
from jax.experimental.pallas import tpu as pltpu

F32 = jnp.float32
BF16 = jnp.bfloat16

EPS = 1e-6
GRID_W = 64
HEAD_DIM = 64
ATT_HEADS = 8
ATT_KV_HEADS = 2
ATT_GROUP = ATT_HEADS // ATT_KV_HEADS
ATT_WIDTH = ATT_HEADS * HEAD_DIM
KV_WIDTH = ATT_KV_HEADS * HEAD_DIM
ROPE_THETA = 10000.0
ROPE_AXIS_DIM = HEAD_DIM // 2
HGRN_WIDTH = 256
HGRN_HEADS = 4
HGRN_CHUNK = 16
CONV_WIDTH = 256
CONV_KERNEL = 31
CONV_PAD = (CONV_KERNEL - 1) // 2
CONV_HALO = 16
V_ROWS = 2 * HEAD_DIM
KV_UNROLL = 8

LANES = 128
SUBLANES = 8
LOG2E = 1.4426950408889634
VMEM_LIMIT = 56 * 1024 * 1024

_C_Q = 0
_C_K = _C_Q + ATT_WIDTH
_C_V = _C_K + KV_WIDTH
_C_HQ = _C_V + KV_WIDTH
_C_HI = _C_HQ + HGRN_WIDTH
_C_FF = _C_HI + HGRN_WIDTH
_C_FB = _C_FF + HGRN_WIDTH
_C_HG = _C_FB + HGRN_WIDTH
_C_CA = _C_HG + HGRN_WIDTH
_C_CG = _C_CA + CONV_WIDTH
IN_COLS = _C_CG + CONV_WIDTH


def _dot(a, b):
    return jnp.dot(a, b, preferred_element_type=F32)


def _split_dot(x, m, parts):
    acc = None
    r = x
    for _ in range(parts):
        t = r.astype(BF16)
        acc = _dot(t, m) if acc is None else acc + _dot(t, m)
        r = r - t.astype(F32)
    return acc


def _group_matrix(width, group, value):
    r = lax.broadcasted_iota(jnp.int32, (width, width), 0) // group
    c = lax.broadcasted_iota(jnp.int32, (width, width), 1) // group
    return jnp.where(r == c, value, 0.0).astype(BF16)


def _group_mean_matrix(width, group):
    return _group_matrix(width, group, 1.0 / group)


def _sigmoid(x):
    return 1.0 / (1.0 + jnp.exp(-x))


def _in_proj_kernel(x_ref, g_ref, w_ref, qn_ref, kn_ref, cos_ref, sin_ref, lbf_ref, lbb_ref,
                    qt_out, k_out, vt_out, hq_out, hv_out, lff_out, kf_out, lfb_out, kb_out, sg_out, u_out,
                    *, layer):
    x = x_ref[...]
    ms = jnp.mean(x * x, axis=-1, keepdims=True)
    h = (x * lax.rsqrt(ms + EPS) * g_ref[...]).astype(BF16)

    def proj(lo, width):
        return _dot(h, w_ref[:, lo:lo + width])

    gmat = _group_mean_matrix(LANES, HEAD_DIM)
    cos = cos_ref[...]
    sin = sin_ref[...]
    lane = lax.broadcasted_iota(jnp.int32, (1, LANES), 1)
    first_half = (lane % ROPE_AXIS_DIM) < (ROPE_AXIS_DIM // 2)

    def norm_rope(z, gain, scale):
        zn = z * lax.rsqrt(_split_dot(z * z, gmat, 2) + EPS) * gain
        half = ROPE_AXIS_DIM // 2
        partner = jnp.where(first_half, pltpu.roll(zn, LANES - half, 1), pltpu.roll(zn, half, 1))
        return (zn * cos + partner * sin) * scale

    zq = proj(_C_Q, ATT_WIDTH)
    q_scale = HEAD_DIM ** -0.5 * LOG2E
    for c in range(ATT_WIDTH // LANES):
        qc = norm_rope(zq[:, c * LANES:(c + 1) * LANES], qn_ref[...], q_scale)
        qt_out[c * LANES:(c + 1) * LANES, :] = jnp.transpose(qc).astype(BF16)

    zk = norm_rope(proj(_C_K, KV_WIDTH), kn_ref[...], 1.0).astype(BF16)
    for hh in range(ATT_KV_HEADS):
        k_out[hh] = zk[:, hh * HEAD_DIM:(hh + 1) * HEAD_DIM]

    zvt = jnp.transpose(proj(_C_V, KV_WIDTH)).astype(BF16)
    ones = jnp.ones((V_ROWS - HEAD_DIM, zvt.shape[1]), BF16)
    for hh in range(ATT_KV_HEADS):
        vt_out[hh] = jnp.concatenate([zvt[hh * HEAD_DIM:(hh + 1) * HEAD_DIM, :], ones], axis=0)

    def lower_bound(lb_ref):
        p = lb_ref[...]
        p = jnp.exp(p - jnp.max(p, axis=0, keepdims=True))
        p = p / jnp.sum(p, axis=0, keepdims=True)
        lb = jnp.zeros((1, HGRN_WIDTH), F32)
        for i in range(1, layer + 1):
            lb = lb + p[i:i + 1, :]
        return lb

    def forget(z, lb):
        t = jnp.exp(-z)
        sig = 1.0 / (1.0 + t)
        return jnp.log(lb + (1.0 - lb) * sig), (1.0 - lb) * (t * sig)

    hq_out[...] = proj(_C_HQ, HGRN_WIDTH)
    hv_out[...] = proj(_C_HI, HGRN_WIDTH)
    lf, kk = forget(proj(_C_FF, HGRN_WIDTH), lower_bound(lbf_ref))
    lff_out[...] = lf
    kf_out[...] = kk
    lf, kk = forget(proj(_C_FB, HGRN_WIDTH), lower_bound(lbb_ref))
    lfb_out[...] = lf
    kb_out[...] = kk
    zg = proj(_C_HG, HGRN_WIDTH)
    sg_out[...] = zg * _sigmoid(zg)

    u_out[...] = proj(_C_CA, CONV_WIDTH) * _sigmoid(proj(_C_CG, CONV_WIDTH))


def _in_proj(x, g, w, qn, kn, cos, sin, lbf, lbb, *, layer, tm):
    B, L, D = x.shape
    nt = L // tm
    tok = lambda width: pl.BlockSpec((None, tm, width), lambda b, i: (b, i, 0))
    full = lambda a: pl.BlockSpec(a.shape, lambda b, i: (0,) * a.ndim)
    hg = jax.ShapeDtypeStruct((B, L, HGRN_WIDTH), F32)
    out_shape = (
        jax.ShapeDtypeStruct((B, ATT_WIDTH, L), BF16),
        jax.ShapeDtypeStruct((B, ATT_KV_HEADS, L, HEAD_DIM), BF16),
        jax.ShapeDtypeStruct((B, ATT_KV_HEADS, V_ROWS, L), BF16),
        hg, hg, hg, hg, hg, hg, hg,
        jax.ShapeDtypeStruct((B, L, CONV_WIDTH), F32),
    )
    out_specs = (
        pl.BlockSpec((None, ATT_WIDTH, tm), lambda b, i: (b, 0, i)),
        pl.BlockSpec((None, ATT_KV_HEADS, tm, HEAD_DIM), lambda b, i: (b, 0, i, 0)),
        pl.BlockSpec((None, ATT_KV_HEADS, V_ROWS, tm), lambda b, i: (b, 0, 0, i)),
        tok(HGRN_WIDTH), tok(HGRN_WIDTH), tok(HGRN_WIDTH), tok(HGRN_WIDTH), tok(HGRN_WIDTH), tok(HGRN_WIDTH),
        tok(HGRN_WIDTH), tok(CONV_WIDTH),
    )
    in_specs = [
        tok(D), full(g), full(w), full(qn), full(kn),
        pl.BlockSpec((tm, LANES), lambda b, i: (i, 0)),
        pl.BlockSpec((tm, LANES), lambda b, i: (i, 0)),
        full(lbf), full(lbb),
    ]
    return pl.pallas_call(
        functools.partial(_in_proj_kernel, layer=layer),
        grid=(B, nt),
        in_specs=in_specs,
        out_specs=out_specs,
        out_shape=out_shape,
        compiler_params=pltpu.CompilerParams(
            dimension_semantics=("parallel", "parallel"), vmem_limit_bytes=VMEM_LIMIT),
        name=f"in_proj_{layer}",
    )(x, g, w, qn, kn, cos, sin, lbf, lbb)


def _attn_kernel(qt_ref, k_ref, vt_ref, o_ref, qs_ref, s_ref, smax_ref, m_ref, acc_ref, *, tq, tk, nk):
    qt = qt_ref[...]
    qs_ref[...] = jnp.concatenate(
        [qt[g * HEAD_DIM:(g + 1) * HEAD_DIM, :] for g in range(ATT_GROUP)], axis=1)
    m_ref[...] = jnp.full(m_ref.shape, -jnp.inf, F32)
    acc_ref[...] = jnp.zeros(acc_ref.shape, F32)

    def scores(j, g, slot):
        k0 = pl.multiple_of(j * tk, tk)
        cols = slice(g * tq, (g + 1) * tq)
        s = _dot(k_ref[pl.ds(k0, tk), :], qs_ref[:, cols])
        s_ref[:, cols] = s
        smax_ref[slot, :, cols] = jnp.max(s, axis=0, keepdims=True)

    def step(j, slot, last):
        k0 = pl.multiple_of(j * tk, tk)
        m_prev = m_ref[...]
        m_new = jnp.maximum(m_prev, smax_ref[slot])
        alpha = jnp.exp2(m_prev - m_new)
        vt = vt_ref[:, pl.ds(k0, tk)]
        for g in range(ATT_GROUP):
            cols = slice(g * tq, (g + 1) * tq)
            p = jnp.exp2(s_ref[:, cols] - m_new[:, cols]).astype(BF16)
            acc_ref[:, cols] = alpha[:, cols] * acc_ref[:, cols] + _dot(vt, p)
            if not last:
                scores(j + 1, g, 1 - slot)
        m_ref[...] = m_new

    for g in range(ATT_GROUP):
        scores(0, g, 0)

    def body(jj, carry):
        for u in range(KV_UNROLL):
            step(KV_UNROLL * jj + u, u % 2, False)
        return carry

    lax.fori_loop(0, nk // KV_UNROLL - 1, body, 0)
    for u in range(KV_UNROLL):
        step(nk - KV_UNROLL + u, u % 2, u == KV_UNROLL - 1)
    res = acc_ref[0:HEAD_DIM, :] / acc_ref[HEAD_DIM:HEAD_DIM + 1, :]
    o_ref[...] = jnp.concatenate(
        [jnp.transpose(res[:, g * tq:(g + 1) * tq]) for g in range(ATT_GROUP)], axis=1).astype(o_ref.dtype)


def _attention(qt, k, vat, *, tq, tk):
    B, _, L = qt.shape
    gw = ATT_GROUP * HEAD_DIM
    mq = ATT_GROUP * tq
    assert (L // tk) % KV_UNROLL == 0, "the kv loop handles KV_UNROLL chunks per trip"
    return pl.pallas_call(
        functools.partial(_attn_kernel, tq=tq, tk=tk, nk=L // tk),
        grid=(B, ATT_KV_HEADS, L // tq),
        in_specs=[
            pl.BlockSpec((None, gw, tq), lambda b, h, i: (b, h, i)),
            pl.BlockSpec((None, None, L, HEAD_DIM), lambda b, h, i: (b, h, 0, 0)),
            pl.BlockSpec((None, None, V_ROWS, L), lambda b, h, i: (b, h, 0, 0)),
        ],
        out_specs=pl.BlockSpec((None, tq, gw), lambda b, h, i: (b, i, h)),
        out_shape=jax.ShapeDtypeStruct((B, L, ATT_WIDTH), BF16),
        scratch_shapes=[
            pltpu.VMEM((HEAD_DIM, mq), BF16),
            pltpu.VMEM((tk, mq), F32),
            pltpu.VMEM((2, 1, mq), F32),
            pltpu.VMEM((1, mq), F32),
            pltpu.VMEM((V_ROWS, mq), F32),
        ],
        compiler_params=pltpu.CompilerParams(
            dimension_semantics=("parallel", "parallel", "parallel"), vmem_limit_bytes=VMEM_LIMIT),
        name="attention",
    )(qt, k, vat)


def _hgrn_kernel(q_ref, v_ref, lf_ref, k_ref, o_ref, s_ref, *, T, reverse):
    C = HGRN_CHUNK
    W = HGRN_WIDTH

    @pl.when(pl.program_id(1) == 0)
    def _():
        s_ref[...] = jnp.zeros(s_ref.shape, F32)

    q = q_ref[...]
    v = v_ref[...]
    k = k_ref[...]
    lf = lf_ref[...]

    r = lax.broadcasted_iota(jnp.int32, (T, T), 0)
    c = lax.broadcasted_iota(jnp.int32, (T, T), 1)
    same = (r // C) == (c // C)
    tri = same & ((c >= r) if reverse else (c <= r))
    m_tri = jnp.where(tri, 1.0, 0.0).astype(BF16)
    m_all = jnp.where(same, 1.0, 0.0).astype(BF16)

    def left_split_dot(m, x):
        acc = None
        rem = x
        for _ in range(3):
            t = rem.astype(BF16)
            acc = _dot(m, t) if acc is None else acc + _dot(m, t)
            rem = rem - t.astype(F32)
        return acc

    b = left_split_dot(m_tri, lf * LOG2E)
    bl = left_split_dot(m_all, lf * LOG2E)

    zpad = jnp.zeros((SUBLANES, W), F32)
    padded = [jnp.concatenate([a, zpad] if reverse else [zpad, a], axis=0) for a in (k, b, v)]
    pos = lax.broadcasted_iota(jnp.int32, (T, 1), 0) % C
    head_sum = _group_matrix(W, W // HGRN_HEADS, 1.0)
    acc = jnp.zeros((T, W), F32)
    for r in range(SUBLANES):
        if r == 0:
            shifted = padded
        else:
            shifted = [pltpu.roll(a, (T + SUBLANES - r) if reverse else r, 0) for a in padded]
        for d in range(r, C, SUBLANES):
            near = (d == r) != reverse
            lo = SUBLANES if near else 0
            ks, bs, vs = (a[lo:lo + T, :] for a in shifted)
            valid = (pos <= C - 1 - d) if reverse else (pos >= d)
            w = jnp.where(valid, q * ks * jnp.exp2(b - bs), 0.0)
            acc = acc + _dot(w.astype(BF16), head_sum) * vs

    qt = (q * jnp.exp2(b)).astype(BF16)
    kt = (k * jnp.exp2(bl - b)).astype(BF16)
    dec = jnp.exp2(bl)
    vb = v.astype(BF16)
    dh = W // HGRN_HEADS
    lane = lax.broadcasted_iota(jnp.int32, (1, W), 1)
    hmask = [(lane // dh) == hh for hh in range(HGRN_HEADS)]
    zero = jnp.zeros((), BF16)

    s = s_ref[...]
    order = range(T // C - 1, -1, -1) if reverse else range(T // C)
    for n in order:
        r0 = n * C
        qn = qt[r0:r0 + C]
        kn = kt[r0:r0 + C]
        vn = vb[r0:r0 + C]
        qexp = jnp.concatenate([jnp.where(hmask[hh], qn, zero) for hh in range(HGRN_HEADS)], axis=0)
        o4 = lax.dot_general(qexp, s.astype(BF16), (((1,), (1,)), ((), ())), preferred_element_type=F32)
        on = jnp.concatenate([o4[hh * C:(hh + 1) * C, :] for hh in range(HGRN_HEADS)], axis=1)
        o_ref[r0:r0 + C, :] = acc[r0:r0 + C] + on
        kexp = jnp.concatenate([jnp.where(hmask[hh], kn, zero) for hh in range(HGRN_HEADS)], axis=0)
        vexp = jnp.concatenate([vn[:, hh * dh:(hh + 1) * dh] for hh in range(HGRN_HEADS)], axis=0)
        u = lax.dot_general(vexp, kexp, (((0,), (0,)), ((), ())), preferred_element_type=F32)
        s = s * dec[r0:r0 + 1, :] + u
    s_ref[...] = s


def _hgrn(q, v, lf, k, *, T, reverse):
    B, L, W = q.shape
    nt = L // T
    idx = (lambda b, i: (b, nt - 1 - i, 0)) if reverse else (lambda b, i: (b, i, 0))
    spec = pl.BlockSpec((None, T, W), idx)
    return pl.pallas_call(
        functools.partial(_hgrn_kernel, T=T, reverse=reverse),
        grid=(B, nt),
        in_specs=[spec, spec, spec, spec],
        out_specs=spec,
        out_shape=jax.ShapeDtypeStruct((B, L, W), F32),
        scratch_shapes=[pltpu.VMEM((W // HGRN_HEADS, W), F32)],
        compiler_params=pltpu.CompilerParams(
            dimension_semantics=("parallel", "arbitrary"), vmem_limit_bytes=VMEM_LIMIT),
        name="hgrn_bwd" if reverse else "hgrn_fwd",
    )(q, v, lf, k)


def _out_mlp_kernel(x_ref, oa_ref, ofw_ref, obw_ref, sg_ref, hn_ref, up_ref, uc_ref, un_ref,
                    cw_ref, cb_ref, lg_ref, lb_ref, wo_ref, nm_ref, w1_ref, w2_ref, out_ref, ext_ref, stage_ref,
                    *, tm, sub, ff_chunk):
    i = pl.program_id(1)
    nt = pl.num_programs(1)

    ext_ref[0:CONV_HALO, :] = jnp.where(i > 0, up_ref[...], 0.0)
    ext_ref[CONV_HALO:CONV_HALO + tm, :] = uc_ref[...]
    ext_ref[CONV_HALO + tm:, :] = jnp.where(i < nt - 1, un_ref[...], 0.0)
    gmat = _group_mean_matrix(HGRN_WIDTH, HGRN_WIDTH // HGRN_HEADS)

    def mixer_inputs(r0, n):
        rows = slice(r0, r0 + n)
        o = ofw_ref[rows, :] + obw_ref[rows, :]
        o_hg = o * lax.rsqrt(_split_dot(o * o, gmat, 2) + EPS) * hn_ref[...] * sg_ref[rows, :]

        first = CONV_HALO - CONV_PAD
        y = None
        for r in range(SUBLANES):
            z = None
            for w in range(CONV_KERNEL):
                if (first + w) % SUBLANES == r:
                    term = ext_ref[pl.ds(r0 + first + w - r, n + SUBLANES), :] * cw_ref[w:w + 1, :]
                    z = term if z is None else z + term
            if r:
                z = pltpu.roll(z, n + SUBLANES - r, 0)
            y = z[:n, :] if y is None else y + z[:n, :]
        y = y + cb_ref[...]
        mu = jnp.mean(y, axis=-1, keepdims=True)
        yc = y - mu
        yn = yc * lax.rsqrt(jnp.mean(yc * yc, axis=-1, keepdims=True) + EPS) * lg_ref[...] + lb_ref[...]
        o_cv = yn * _sigmoid(yn)
        stage_ref[rows, 0:HGRN_WIDTH] = o_hg.astype(BF16)
        stage_ref[rows, HGRN_WIDTH:] = o_cv.astype(BF16)

    def project_mlp(r0):
        rows = slice(r0, r0 + sub)
        mix = (_dot(oa_ref[rows, :], wo_ref[0:ATT_WIDTH, :])
               + _dot(stage_ref[rows, :], wo_ref[ATT_WIDTH:, :]))
        x1 = x_ref[rows, :] + mix

        ms = jnp.mean(x1 * x1, axis=-1, keepdims=True)
        hn = (x1 * lax.rsqrt(ms + EPS) * nm_ref[...]).astype(BF16)
        mlp = None
        d_ff = w1_ref.shape[1]
        for c0 in range(0, d_ff, ff_chunk):
            a = jnp.maximum(_dot(hn, w1_ref[:, c0:c0 + ff_chunk]), 0.0)
            part = _dot((a * a).astype(BF16), w2_ref[c0:c0 + ff_chunk, :])
            mlp = part if mlp is None else mlp + part
        out_ref[rows, :] = x1 + mlp

    for r0 in range(0, tm, sub):
        mixer_inputs(r0, sub)
    for r0 in range(0, tm, sub):
        project_mlp(r0)


def _out_mlp(x, oa, ofw, obw, sg, hn_g, u, cw, cb, lg, lb, wo, nm, w1, w2, *, tm):
    B, L, D = x.shape
    nt = L // tm
    hb = tm // CONV_HALO
    nhb = L // CONV_HALO
    tok = lambda width: pl.BlockSpec((None, tm, width), lambda b, i: (b, i, 0))
    const = lambda a: pl.BlockSpec(a.shape, lambda b, i: (0,) * a.ndim, pipeline_mode=pl.Buffered(1))
    in_specs = [
        tok(D), tok(ATT_WIDTH), tok(HGRN_WIDTH), tok(HGRN_WIDTH), tok(HGRN_WIDTH), const(hn_g),
        pl.BlockSpec((None, CONV_HALO, CONV_WIDTH), lambda b, i: (b, jnp.maximum(i * hb - 1, 0), 0)),
        tok(CONV_WIDTH),
        pl.BlockSpec((None, CONV_HALO, CONV_WIDTH), lambda b, i: (b, jnp.minimum((i + 1) * hb, nhb - 1), 0)),
        const(cw), const(cb), const(lg), const(lb), const(wo), const(nm), const(w1), const(w2),
    ]
    return pl.pallas_call(
        functools.partial(_out_mlp_kernel, tm=tm, sub=min(512, tm), ff_chunk=min(1024, w1.shape[1])),
        grid=(B, nt),
        in_specs=in_specs,
        out_specs=tok(D),
        out_shape=jax.ShapeDtypeStruct((B, L, D), F32),
        scratch_shapes=[pltpu.VMEM((tm + 2 * CONV_HALO, CONV_WIDTH), F32),
                        pltpu.VMEM((tm, HGRN_WIDTH + CONV_WIDTH), BF16)],
        compiler_params=pltpu.CompilerParams(
            dimension_semantics=("parallel", "parallel"), vmem_limit_bytes=VMEM_LIMIT),
        name="out_mlp",
    )(x, oa, ofw, obw, sg, hn_g, u, u, u, cw, cb, lg, lb, wo, nm, w1, w2)


def _rope_tables(L):
    rows = L // GRID_W
    row_idx = jnp.repeat(jnp.arange(rows), GRID_W)
    col_idx = jnp.tile(jnp.arange(GRID_W), rows)
    inv_freq = ROPE_THETA ** (-jnp.arange(0, ROPE_AXIS_DIM, 2, dtype=F32) / ROPE_AXIS_DIM)
    ar = row_idx.astype(F32)[:, None] * inv_freq[None, :]
    ac = col_idx.astype(F32)[:, None] * inv_freq[None, :]
    cos = jnp.concatenate([jnp.cos(ar), jnp.cos(ar), jnp.cos(ac), jnp.cos(ac)], axis=-1)
    sin = jnp.concatenate([-jnp.sin(ar), jnp.sin(ar), -jnp.sin(ac), jnp.sin(ac)], axis=-1)
    reps = LANES // HEAD_DIM
    return jnp.tile(cos, (1, reps)), jnp.tile(sin, (1, reps))


def _tile(L, want):
    t = min(want, L)
    assert L % t == 0, (L, t)
    return t


def kernel(x, w_in, w_out, norm_mix, norm_mlp, q_norm, k_norm, hgrn_lb_fwd, hgrn_lb_bwd, hgrn_norm,
           conv_w, conv_b, conv_ln_g, conv_ln_b, w_mlp_in, w_mlp_out):
    B, L, D = x.shape
    depth = w_in.shape[0]
    assert w_in.shape[2] == IN_COLS and L % GRID_W == 0
    cos, sin = _rope_tables(L)
    row = lambda a: a.reshape(1, -1).astype(F32)
    reps = LANES // HEAD_DIM
    tm_in = _tile(L, 512)
    tq = _tile(L, 512)
    tk = _tile(L // KV_UNROLL, 512)
    t_hg = _tile(L, 256)
    tm_out = _tile(L, 1024)

    for l in range(depth):
        (q, kt, va, hq, hv, lff, kf, lfb, kb, sg, u) = _in_proj(
            x, row(norm_mix[l]), w_in[l].astype(BF16),
            jnp.tile(row(q_norm[l]), (1, reps)), jnp.tile(row(k_norm[l]), (1, reps)),
            cos, sin, hgrn_lb_fwd.astype(F32), hgrn_lb_bwd.astype(F32), layer=l, tm=tm_in)
        o_att = _attention(q, kt, va, tq=tq, tk=tk)
        o_fw = _hgrn(hq, hv, lff, kf, T=t_hg, reverse=False)
        o_bw = _hgrn(hq, hv, lfb, kb, T=t_hg, reverse=True)
        x = _out_mlp(
            x, o_att, o_fw, o_bw, sg, row(hgrn_norm[l]), u,
            conv_w[l].astype(F32), row(conv_b[l]), row(conv_ln_g[l]), row(conv_ln_b[l]),
            w_out[l].astype(BF16), row(norm_mlp[l]), w_mlp_in[l].astype(BF16), w_mlp_out[l].astype(BF16),
            tm=tm_out)
    return x
```
